```python
import jax, jax.numpy as jnp
from jax import lax
import numpy as np

D_MODEL = 2048
BATCH = 4
SEQ = 8192
DEPTH = 1
DEC_BATCH = 1
DEC_SEQ = 16384
PAST_LEN = 128

CHUNK = 128
G_GROUPS = 8
G_WIDTH = D_MODEL
G_GDIM = G_WIDTH // G_GROUPS
M_HEADS = 8
M_WIDTH = D_MODEL
M_HDIM = M_WIDTH // M_HEADS
CONV_W = 5
N_GROUPS = 4
EXPERTS_PER_GROUP = 8
N_EXPERTS = N_GROUPS * EXPERTS_PER_GROUP
TOP_K_IN_GROUP = 2
D_EXPERT = 1024
MOE_BLOCK = 128
EPS = 1e-6

OFF_GU = 0
OFF_GV = OFF_GU + G_WIDTH
OFF_Q = OFF_GV + G_WIDTH
OFF_K = OFF_Q + M_WIDTH
OFF_V = OFF_K + M_WIDTH
OFF_O = OFF_V + M_WIDTH
OFF_CG = OFF_O + M_WIDTH
N_CELL_GATES = 4 * M_HEADS
OFF_MERGE = OFF_CG + N_CELL_GATES
D_IN = OFF_MERGE + 2 * D_MODEL

kernel_name = "hybrid_gmlp_mlstm_hmoe_encoder"


def rmsnorm(x, g):
    xf = x.astype(jnp.float32)
    y = xf * lax.rsqrt(jnp.mean(xf * xf, axis=-1, keepdims=True) + EPS)
    return (y * g.astype(jnp.float32)).astype(x.dtype)


def gmlp_branch(u, v, ln_g, w_s, b_s):
    B, S, _ = u.shape
    vf = v.astype(jnp.float32)
    mu = jnp.mean(vf, axis=-1, keepdims=True)
    var = jnp.mean(jnp.square(vf - mu), axis=-1, keepdims=True)
    vn = ((vf - mu) * lax.rsqrt(var + EPS) * ln_g.astype(jnp.float32)).astype(u.dtype)
    vc = vn.reshape(B, S // CHUNK, CHUNK, G_GROUPS, G_GDIM)
    s = jnp.einsum('gts,bcsgd->bctgd', w_s, vc) + b_s.T[:, :, None]
    return u * s.reshape(B, S, G_WIDTH)


def centred_dwconv(x, w):
    C = x.shape[-1]
    return lax.conv_general_dilated(
        x, w.astype(x.dtype)[:, None, :], window_strides=(1,),
        padding=[(CONV_W // 2, CONV_W // 2)],
        dimension_numbers=('NWC', 'WIO', 'NWC'), feature_group_count=C)


def mlstm_scan(q, k, v, log_i, log_f):
    B, H, S, D = q.shape
    nc = S // CHUNK
    to_chunks = lambda a: jnp.moveaxis(a.reshape(B, H, nc, CHUNK, *a.shape[3:]), 2, 0)
    tril = jnp.tril(jnp.ones((CHUNK, CHUNK), dtype=bool))

    def step(carry, inp):
        C, n, m = carry
        qc, kc, vc, lic, lfc = inp
        F = jnp.cumsum(lfc, axis=-1)
        Dm = F[..., :, None] - F[..., None, :] + lic[..., None, :]
        Dm = jnp.where(tril, Dm, -jnp.inf)
        inter = F + m[..., None]
        m_t = jnp.maximum(inter, jnp.max(Dm, axis=-1))
        w_inter = jnp.exp(inter - m_t)
        Sm = jnp.einsum('bhtd,bhsd->bhts', qc, kc) * jnp.exp(Dm - m_t[..., None])
        num = (w_inter[..., None] * jnp.einsum('bhtd,bhde->bhte', qc, C)
               + jnp.einsum('bhts,bhse->bhte', Sm, vc))
        den = w_inter * jnp.einsum('bhtd,bhd->bht', qc, n) + jnp.sum(Sm, axis=-1)
        h = num / jnp.maximum(jnp.abs(den), jnp.exp(-m_t))[..., None]
        F_L = F[..., -1]
        g_s = F_L[..., None] - F + lic
        m_new = jnp.maximum(F_L + m, jnp.max(g_s, axis=-1))
        a = jnp.exp(F_L + m - m_new)
        kw = kc * jnp.exp(g_s - m_new[..., None])[..., None]
        C_new = a[..., None, None] * C + jnp.einsum('bhsd,bhse->bhde', kw, vc)
        n_new = a[..., None] * n + jnp.sum(kw, axis=2)
        return (C_new, n_new, m_new), h

    init = (jnp.zeros((B, H, D, D), jnp.float32), jnp.zeros((B, H, D), jnp.float32),
            jnp.zeros((B, H), jnp.float32))
    _, hs = lax.scan(step, init, (to_chunks(q), to_chunks(k), to_chunks(v),
                                  to_chunks(log_i), to_chunks(log_f)))
    return jnp.moveaxis(hs, 0, 2).reshape(B, H, S, D)


def mlstm_branch(q_raw, k_raw, v_raw, o_raw, cg_raw, conv_w, b_cg, head_g):
    B, S, _ = q_raw.shape
    qk = jax.nn.silu(centred_dwconv(jnp.concatenate([q_raw, k_raw], axis=-1), conv_w))
    heads = lambda a: a.astype(jnp.float32).reshape(B, S, M_HEADS, M_HDIM).transpose(0, 2, 1, 3)
    q = heads(qk[..., :M_WIDTH])
    k = heads(qk[..., M_WIDTH:]) * (M_HDIM ** -0.5)
    v = heads(v_raw)
    g = (cg_raw.astype(jnp.float32) + b_cg.astype(jnp.float32)).reshape(B, S, 4, M_HEADS)
    g = g.transpose(2, 0, 3, 1)
    li_f, lf_f = g[0], jax.nn.log_sigmoid(g[1])
    li_b, lf_b = g[2], jax.nn.log_sigmoid(g[3])
    h_f = mlstm_scan(q, k, v, li_f, lf_f)
    flip = lambda a: jnp.flip(a, axis=2)
    h_b = flip(mlstm_scan(flip(q), flip(k), flip(v), flip(li_b), flip(lf_b)))
    h = h_f + h_b
    h = h * lax.rsqrt(jnp.mean(h * h, axis=-1, keepdims=True) + EPS) * head_g.astype(jnp.float32)[None, :, None, :]
    h = h.transpose(0, 2, 1, 3).reshape(B, S, M_WIDTH).astype(q_raw.dtype)
    return h * jax.nn.sigmoid(o_raw)


def token_mixer(xn, w_in, b_cg, gmlp_ln_g, gmlp_w_s, gmlp_b_s, conv_w, head_g, w_out):
    z = xn @ w_in
    gu = jax.nn.gelu(z[..., OFF_GU:OFF_GV])
    gv = jax.nn.gelu(z[..., OFF_GV:OFF_Q])
    a = gmlp_branch(gu, gv, gmlp_ln_g, gmlp_w_s, gmlp_b_s)
    b = mlstm_branch(z[..., OFF_Q:OFF_K], z[..., OFF_K:OFF_V], z[..., OFF_V:OFF_O],
                     z[..., OFF_O:OFF_CG], z[..., OFF_CG:OFF_MERGE], conv_w, b_cg, head_g)
    gate_a = jax.nn.sigmoid(z[..., OFF_MERGE:OFF_MERGE + D_MODEL])
    gate_b = jax.nn.sigmoid(z[..., OFF_MERGE + D_MODEL:])
    return (gate_a * a + gate_b * b) @ w_out


def hier_moe(x, w_rg, b_rg, w_re, b_re, w_gate, w_up, w_down):
    B, S, Dm = x.shape
    T = B * S
    xt = x.reshape(T, Dm)
    g_logits = (xt @ w_rg).astype(jnp.float32) + b_rg.astype(jnp.float32)
    g_prob = jax.nn.softmax(g_logits, axis=-1)
    _, g_sel = lax.top_k(g_logits, 1)
    p_g = jnp.take_along_axis(g_prob, g_sel, axis=1)
    e_logits = ((xt @ w_re).astype(jnp.float32) + b_re.astype(jnp.float32)).reshape(T, N_GROUPS, EXPERTS_PER_GROUP)
    e_in = jnp.take_along_axis(e_logits, g_sel[:, :, None], axis=1)[:, 0]
    top_v, top_i = lax.top_k(e_in, TOP_K_IN_GROUP)
    weights = p_g * jax.nn.softmax(top_v, axis=-1)
    expert_id = g_sel * EXPERTS_PER_GROUP + top_i

    N = T * TOP_K_IN_GROUP
    flat_e = expert_id.reshape(N)
    flat_tok = jnp.arange(N, dtype=jnp.int32) // TOP_K_IN_GROUP
    flat_w = weights.reshape(N)
    order = jnp.argsort(flat_e)
    se, stok, sw = flat_e[order], flat_tok[order], flat_w[order]
    counts = jnp.bincount(flat_e, length=N_EXPERTS)
    padded = ((counts + MOE_BLOCK - 1) // MOE_BLOCK) * MOE_BLOCK
    pad_end = jnp.cumsum(padded)
    pad_start = pad_end - padded
    raw_start = jnp.cumsum(counts) - counts
    dest = pad_start[se] + (jnp.arange(N, dtype=jnp.int32) - raw_start[se])
    n_blocks = (N + N_EXPERTS * (MOE_BLOCK - 1) + MOE_BLOCK - 1) // MOE_BLOCK
    buf = jnp.zeros((n_blocks * MOE_BLOCK, Dm), x.dtype).at[dest].set(xt[stok])
    block_e = jnp.searchsorted(pad_end, jnp.arange(n_blocks, dtype=jnp.int32) * MOE_BLOCK, side='right')
    block_e = jnp.minimum(block_e, N_EXPERTS - 1)

    def expert_block(args):
        xb, e = args
        h = jax.nn.silu(xb @ w_gate[e]) * (xb @ w_up[e])
        return h @ w_down[e]

    out = lax.map(expert_block, (buf.reshape(n_blocks, MOE_BLOCK, Dm), block_e))
    out = out.reshape(n_blocks * MOE_BLOCK, Dm)
    y_assign = out[dest] * sw[:, None].astype(x.dtype)
    y = jax.ops.segment_sum(y_assign, stok, num_segments=T)
    return y.reshape(B, S, Dm)


def trunk(x, norm_mix_g, norm_ffn_g, norm_final_g, w_in, b_cell_gates, gmlp_ln_g, gmlp_w_s,
          gmlp_b_s, mlstm_conv_w, mlstm_head_g, w_out, w_router_group, b_router_group,
          w_router_expert, b_router_expert, w_exp_gate, w_exp_up, w_exp_down):
    for l in range(DEPTH):
        h = rmsnorm(x, norm_mix_g[l])
        x = x + token_mixer(h, w_in[l], b_cell_gates[l], gmlp_ln_g[l], gmlp_w_s[l], gmlp_b_s[l],
                            mlstm_conv_w[l], mlstm_head_g[l], w_out[l])
        h = rmsnorm(x, norm_ffn_g[l])
        x = x + hier_moe(h, w_router_group[l], b_router_group[l], w_router_expert[l],
                         b_router_expert[l], w_exp_gate[l], w_exp_up[l], w_exp_down[l])
    return rmsnorm(x, norm_final_g)


def setup_inputs(seed: int = 0) -> dict:
    key = jax.random.key(seed)
    ks = jax.random.split(key, 24)
    f32 = jnp.float32
    nrm = lambda k, shape, s: jax.random.normal(k, shape, f32) * s
    i_bias = lambda k: nrm(k, (DEPTH, M_HEADS), 0.1)
    f_bias = lambda k: 3.0 + 3.0 * jax.random.uniform(k, (DEPTH, M_HEADS), f32)
    b_cell_gates = jnp.concatenate([i_bias(ks[5]), f_bias(ks[6]), i_bias(ks[7]), f_bias(ks[8])], axis=-1)
    return {
        "x_prompt": nrm(ks[0], (BATCH, SEQ, D_MODEL), 1.0),
        "x_sample": nrm(ks[1], (DEC_BATCH, DEC_SEQ, D_MODEL), 1.0),
        "norm_mix_g": 1.0 + nrm(ks[2], (DEPTH, D_MODEL), 0.01),
        "norm_ffn_g": 1.0 + nrm(ks[3], (DEPTH, D_MODEL), 0.01),
        "norm_final_g": 1.0 + nrm(ks[4], (D_MODEL,), 0.01),
        "w_in": nrm(ks[9], (DEPTH, D_MODEL, D_IN), D_MODEL ** -0.5),
        "b_cell_gates": b_cell_gates,
        "gmlp_ln_g": 1.0 + nrm(ks[10], (DEPTH, G_WIDTH), 0.01),
        "gmlp_w_s": nrm(ks[11], (DEPTH, G_GROUPS, CHUNK, CHUNK), CHUNK ** -0.5),
        "gmlp_b_s": 1.0 + nrm(ks[12], (DEPTH, G_GROUPS, CHUNK), 0.1),
        "mlstm_conv_w": nrm(ks[13], (DEPTH, CONV_W, 2 * M_WIDTH), CONV_W ** -0.5),
        "mlstm_head_g": 1.0 + nrm(ks[14], (DEPTH, M_HEADS, M_HDIM), 0.01),
        "w_out": nrm(ks[15], (DEPTH, D_MODEL, D_MODEL), D_MODEL ** -0.5),
        "w_router_group": nrm(ks[16], (DEPTH, D_MODEL, N_GROUPS), D_MODEL ** -0.5),
        "b_router_group": nrm(ks[17], (DEPTH, N_GROUPS), 0.01),
        "w_router_expert": nrm(ks[18], (DEPTH, D_MODEL, N_EXPERTS), D_MODEL ** -0.5),
        "b_router_expert": nrm(ks[19], (DEPTH, N_EXPERTS), 0.01),
        "w_exp_gate": nrm(ks[20], (DEPTH, N_EXPERTS, D_MODEL, D_EXPERT), D_MODEL ** -0.5),
        "w_exp_up": nrm(ks[21], (DEPTH, N_EXPERTS, D_MODEL, D_EXPERT), D_MODEL ** -0.5),
        "w_exp_down": nrm(ks[22], (DEPTH, N_EXPERTS, D_EXPERT, D_MODEL), D_EXPERT ** -0.5),
    }


def reference(x_prompt, x_sample, norm_mix_g, norm_ffn_g, norm_final_g, w_in, b_cell_gates,
              gmlp_ln_g, gmlp_w_s, gmlp_b_s, mlstm_conv_w, mlstm_head_g, w_out, w_router_group,
              b_router_group, w_router_expert, b_router_expert, w_exp_gate, w_exp_up, w_exp_down):
    y_prompt = trunk(x_prompt, norm_mix_g, norm_ffn_g, norm_final_g, w_in, b_cell_gates, gmlp_ln_g,
                     gmlp_w_s, gmlp_b_s, mlstm_conv_w, mlstm_head_g, w_out, w_router_group,
                     b_router_group, w_router_expert, b_router_expert, w_exp_gate, w_exp_up, w_exp_down)
    y_sample = trunk(x_sample, norm_mix_g, norm_ffn_g, norm_final_g, w_in, b_cell_gates, gmlp_ln_g,
                     gmlp_w_s, gmlp_b_s, mlstm_conv_w, mlstm_head_g, w_out, w_router_group,
                     b_router_group, w_router_expert, b_router_expert, w_exp_gate, w_exp_up, w_exp_down)
    return (y_prompt, y_sample)
```

```python
import functools

import jax
import jax.numpy as jnp
import numpy as np
from jax import lax
from jax.experimental import pallas as pl
from jax.experimental.pallas import tpu as pltpu

F32 = jnp.float32
BF16 = jnp.bfloat16

D_MODEL = 2048
CHUNK = 128
G_GROUPS = 8
G_GDIM = D_MODEL // G_GROUPS
M_HEADS = 8
M_HDIM = D_MODEL // M_HEADS
CONV_W = 5
N_GROUPS = 4
EXPERTS_PER_GROUP = 8
N_EXPERTS = N_GROUPS * EXPERTS_PER_GROUP
D_EXPERT = 1024
EPS = 1e-6
LANES = 128
SUBLANES = 8
N_CELL_GATES = 4 * M_HEADS
OFF_CG = 6 * D_MODEL
OFF_MERGE = OFF_CG + N_CELL_GATES

ZB_GU, ZB_GV, ZB_Q, ZB_K, ZB_V, ZB_O, ZB_MA, ZB_MB = range(8)
Z_COLS = 8 * D_MODEL

V_AUG = M_HDIM + LANES
VMEM_LIMIT = 56 * 1024 * 1024


def _cparams(sem):
    return pltpu.CompilerParams(dimension_semantics=sem, vmem_limit_bytes=VMEM_LIMIT)


def _rms(x, g):
    ms = jnp.mean(x * x, axis=-1, keepdims=True)
    return x * lax.rsqrt(ms + EPS) * g


def _sigmoid(x):
    return 1.0 / (1.0 + jnp.exp(-x))


def _gelu_tanh(x):
    c = np.float32(np.sqrt(2.0 / np.pi))
    return 0.5 * x * (1.0 + jnp.tanh(c * (x + np.float32(0.044715) * (x * x * x))))


def _proj_kernel(xp_ref, xs_ref, g_ref, w_ref, wcg_ref, z_ref, cg_ref, h_ref, *, n_prompt_tiles, tn):
    i = pl.program_id(0)
    j = pl.program_id(1)

    @pl.when(j == 0)
    def _():
        @pl.when(i < n_prompt_tiles)
        def _():
            h_ref[...] = _rms(xp_ref[...], g_ref[...]).astype(BF16)

        @pl.when(i >= n_prompt_tiles)
        def _():
            h_ref[...] = _rms(xs_ref[...], g_ref[...]).astype(BF16)

        cg_ref[...] = jnp.dot(h_ref[...], wcg_ref[...], preferred_element_type=F32)

    acc = jnp.dot(h_ref[...], w_ref[...], preferred_element_type=F32)
    n_gelu = 2 * D_MODEL // tn
    n_plain = 3 * D_MODEL // tn

    @pl.when(j < n_gelu)
    def _():
        z_ref[...] = _gelu_tanh(acc).astype(BF16)

    @pl.when(jnp.logical_and(j >= n_gelu, j < n_gelu + n_plain))
    def _():
        z_ref[...] = acc.astype(BF16)

    @pl.when(j >= n_gelu + n_plain)
    def _():
        z_ref[...] = _sigmoid(acc).astype(BF16)


def _proj(xp, xs, g, w, wcg, tm, tn):
    tp, ts = xp.shape[0], xs.shape[0]
    t = tp + ts
    npt, nst = tp // tm, ts // tm
    kern = functools.partial(_proj_kernel, n_prompt_tiles=npt, tn=tn)
    return pl.pallas_call(
        kern,
        grid=(npt + nst, Z_COLS // tn),
        in_specs=[
            pl.BlockSpec((tm, D_MODEL), lambda i, j: (jnp.minimum(i, npt - 1), 0)),
            pl.BlockSpec((tm, D_MODEL), lambda i, j: (jnp.maximum(i - npt, 0), 0)),
            pl.BlockSpec((1, D_MODEL), lambda i, j: (0, 0)),
            pl.BlockSpec((D_MODEL, tn), lambda i, j: (0, j)),
            pl.BlockSpec((D_MODEL, LANES), lambda i, j: (0, 0)),
        ],
        out_specs=[
            pl.BlockSpec((tm, tn), lambda i, j: (i, j)),
            pl.BlockSpec((tm, LANES), lambda i, j: (i, 0)),
        ],
        out_shape=[
            jax.ShapeDtypeStruct((t, Z_COLS), BF16),
            jax.ShapeDtypeStruct((t, LANES), F32),
        ],
        scratch_shapes=[pltpu.VMEM((tm, D_MODEL), BF16)],
        compiler_params=_cparams(("arbitrary", "arbitrary")),
        name="proj",
    )(xp, xs, g, w, wcg)


def _gmlp_kernel(gu_ref, gv_ref, ma_ref, lng_ref, ws_ref, bst_ref, out_ref, vn_ref, *, rows):
    v = gv_ref[...].astype(F32)
    mu = jnp.mean(v, axis=-1, keepdims=True)
    vc = v - mu
    var = jnp.mean(vc * vc, axis=-1, keepdims=True)
    vn_ref[...] = (vc * lax.rsqrt(var + EPS) * lng_ref[...]).astype(BF16)
    for c in range(rows // CHUNK):
        rs = slice(c * CHUNK, (c + 1) * CHUNK)
        for g in range(G_GROUPS):
            cs = slice(g * G_GDIM, (g + 1) * G_GDIM)
            s = jnp.dot(ws_ref[g], vn_ref[rs, cs], preferred_element_type=F32) + bst_ref[:, g:g + 1]
            a = gu_ref[rs, cs].astype(F32) * s
            out_ref[rs, cs] = (ma_ref[rs, cs].astype(F32) * a).astype(BF16)


def _gmlp(z, lng, ws, bst, rows):
    t = z.shape[0]
    zspec = lambda b: pl.BlockSpec((rows, D_MODEL), lambda i, b=b: (i, b))
    return pl.pallas_call(
        functools.partial(_gmlp_kernel, rows=rows),
        grid=(t // rows,),
        in_specs=[
            zspec(ZB_GU), zspec(ZB_GV), zspec(ZB_MA),
            pl.BlockSpec((1, D_MODEL), lambda i: (0, 0)),
            pl.BlockSpec((G_GROUPS, CHUNK, CHUNK), lambda i: (0, 0, 0)),
            pl.BlockSpec((CHUNK, LANES), lambda i: (0, 0)),
        ],
        out_specs=pl.BlockSpec((rows, D_MODEL), lambda i: (i, 0)),
        out_shape=jax.ShapeDtypeStruct((t, D_MODEL), BF16),
        scratch_shapes=[pltpu.VMEM((rows, D_MODEL), BF16)],
        compiler_params=_cparams(("arbitrary",)),
        name="gmlp",
    )(z, z, z, lng, ws, bst)


HALO = 16


def _conv_kernel(flags_ref, prev_ref, cur_ref, next_ref, w_ref, out_ref, *, rows, k_col_blocks_from):
    i = pl.program_id(0)
    j = pl.program_id(1)
    x = cur_ref[...].astype(F32)
    keep_prev = (flags_ref[i, 0] == 0).astype(F32)
    keep_next = (flags_ref[i, 1] == 0).astype(F32)
    p = prev_ref[...].astype(F32) * keep_prev
    n = next_ref[...].astype(F32) * keep_next
    row = lax.broadcasted_iota(jnp.int32, x.shape, 0)
    w = w_ref[...]
    half = CONV_W // 2
    acc = x * w[half:half + 1, :]
    for d in range(1, half + 1):
        xm = pltpu.roll(x, d, 0)
        for r in range(d):
            xm = jnp.where(row == r, p[HALO - d + r:HALO - d + r + 1, :], xm)
        acc = acc + xm * w[half - d:half - d + 1, :]
        xp = pltpu.roll(x, rows - d, 0)
        for r in range(d):
            xp = jnp.where(row == rows - d + r, n[r:r + 1, :], xp)
        acc = acc + xp * w[half + d:half + d + 1, :]
    y = acc * _sigmoid(acc)
    scale = jnp.where(j >= k_col_blocks_from, np.float32(M_HDIM ** -0.5), np.float32(1.0))
    out_ref[...] = (y * scale).astype(BF16)


def _conv(z, w, flags, rows, cols):
    t = z.shape[0]
    ncb = 2 * D_MODEL // cols
    cb0 = ZB_Q * D_MODEL // cols
    hb = rows // HALO
    nhb = t // HALO
    kern = functools.partial(_conv_kernel, rows=rows, k_col_blocks_from=ncb // 2)
    grid_spec = pltpu.PrefetchScalarGridSpec(
        num_scalar_prefetch=1,
        grid=(t // rows, ncb),
        in_specs=[
            pl.BlockSpec((HALO, cols), lambda i, j, f: (jnp.maximum(i * hb - 1, 0), cb0 + j)),
            pl.BlockSpec((rows, cols), lambda i, j, f: (i, cb0 + j)),
            pl.BlockSpec((HALO, cols), lambda i, j, f: (jnp.minimum((i + 1) * hb, nhb - 1), cb0 + j)),
            pl.BlockSpec((CONV_W + 3, cols), lambda i, j, f: (0, j)),
        ],
        out_specs=pl.BlockSpec((rows, cols), lambda i, j, f: (i, j)),
    )
    return pl.pallas_call(
        kern,
        grid_spec=grid_spec,
        out_shape=jax.ShapeDtypeStruct((t, 2 * D_MODEL), BF16),
        compiler_params=_cparams(("arbitrary", "arbitrary")),
        name="conv",
    )(flags, z, z, z, w)


def _log_sigmoid(x):
    return jnp.minimum(x, 0.0) - jnp.log1p(jnp.exp(-jnp.abs(x)))


def _mlstm_kernel(reset_ref, order_ref, *refs, reverse):
    del order_ref
    if reverse:
        (q_ref, k_ref, v_ref, cg_ref, bias_ref, hf_ref, o_ref, mb_ref, ga_ref, hg_ref,
         out_ref, c_ref, m_ref) = refs
    else:
        q_ref, k_ref, v_ref, cg_ref, bias_ref, out_ref, c_ref, m_ref = refs
    step = pl.program_id(0)

    @pl.when(reset_ref[step] == 1)
    def _():
        c_ref[...] = jnp.zeros_like(c_ref)
        m_ref[...] = jnp.zeros_like(m_ref)

    L = CHUNK
    gates = cg_ref[...] + bias_ref[...]
    gates_t = gates.T
    lf = _log_sigmoid(gates)
    lf_t = _log_sigmoid(gates_t)
    r_i = lax.broadcasted_iota(jnp.int32, (L, L), 0)
    c_i = lax.broadcasted_iota(jnp.int32, (L, L), 1)
    if reverse:
        valid = c_i >= r_i
        i_off, f_off, last = 2 * M_HEADS, 3 * M_HEADS, 0
    else:
        valid = c_i <= r_i
        i_off, f_off, last = 0, M_HEADS, L - 1
    tri = valid.astype(F32)
    hp = lax.Precision.HIGHEST
    f_col = jnp.dot(tri, lf, precision=hp, preferred_element_type=F32)
    f_row = lax.dot_general(lf_t, tri, (((1,), (1,)), ((), ())), precision=hp,
                            preferred_element_type=F32)
    b_rows = gates_t[i_off:i_off + M_HEADS, :] - f_row[f_off:f_off + M_HEADS, :]
    lane = lax.broadcasted_iota(jnp.int32, (L, LANES), 1)
    ones_blk = (lane == 0).astype(BF16)

    for h in range(M_HEADS):
        hs = slice(h * M_HDIM, (h + 1) * M_HDIM)
        q = q_ref[:, hs]
        k = k_ref[:, hs]
        v_aug = jnp.concatenate([v_ref[:, hs], ones_blk], axis=1)
        fc = f_col[:, f_off + h:f_off + h + 1]
        lic = gates[:, i_off + h:i_off + h + 1]
        brow = b_rows[h:h + 1, :]
        f_last = f_row[f_off + h:f_off + h + 1, last:last + 1]
        m_old = m_ref[h][:, 0:1]

        dm = jnp.where(valid, fc + brow, -jnp.inf)
        m_t = jnp.maximum(fc + m_old, jnp.max(dm, axis=-1, keepdims=True))
        s = lax.dot_general(q, k, (((1,), (1,)), ((), ())), preferred_element_type=F32)
        p = (s * jnp.exp(dm - m_t)).astype(BF16)
        w_inter = jnp.exp(fc + m_old - m_t)
        c_old = c_ref[h]
        qc = jnp.dot(q, c_old.astype(BF16), preferred_element_type=F32)
        nd = w_inter * qc + jnp.dot(p, v_aug, preferred_element_type=F32)
        den = nd[:, M_HDIM:M_HDIM + 1]
        hout = nd[:, :M_HDIM] * (1.0 / jnp.maximum(jnp.abs(den), jnp.exp(-m_t)))

        g_col = f_last - fc + lic
        m_new = jnp.maximum(f_last + m_old, jnp.max(f_last + brow, axis=-1, keepdims=True))
        a = jnp.exp(f_last + m_old - m_new)
        kw = (k.astype(F32) * jnp.exp(g_col - m_new)).astype(BF16)
        c_ref[h] = a * c_old + lax.dot_general(kw, v_aug, (((0,), (0,)), ((), ())),
                                               preferred_element_type=F32)
        m_ref[h] = jnp.broadcast_to(m_new, (1, LANES))

        if reverse:
            hsum = hf_ref[:, hs].astype(F32) + hout
            hn = _rms(hsum, hg_ref[h:h + 1, :])
            b = hn * o_ref[:, hs].astype(F32)
            merged = ga_ref[:, hs].astype(F32) + mb_ref[:, hs].astype(F32) * b
            out_ref[:, hs] = merged.astype(BF16)
        else:
            out_ref[:, hs] = hout.astype(BF16)


def _mlstm(qk, z, cg, bias, reset, order, reverse, extras=None):
    t = z.shape[0]
    nchunks = t // CHUNK
    row = lambda b: pl.BlockSpec((CHUNK, D_MODEL), lambda i, rs, od, b=b: (od[i], b))
    in_specs = [row(0), row(1), row(ZB_V),
                pl.BlockSpec((CHUNK, LANES), lambda i, rs, od: (od[i], 0)),
                pl.BlockSpec((1, LANES), lambda i, rs, od: (0, 0))]
    args = [qk, qk, z, cg, bias]
    if reverse:
        hf, ga, hg = extras
        in_specs += [row(0), row(ZB_O), row(ZB_MB), row(0),
                     pl.BlockSpec((M_HEADS, M_HDIM), lambda i, rs, od: (0, 0))]
        args += [hf, z, z, ga, hg]
    grid_spec = pltpu.PrefetchScalarGridSpec(
        num_scalar_prefetch=2,
        grid=(nchunks,),
        in_specs=in_specs,
        out_specs=row(0),
        scratch_shapes=[pltpu.VMEM((M_HEADS, M_HDIM, V_AUG), F32),
                        pltpu.VMEM((M_HEADS, 1, LANES), F32)],
    )
    return pl.pallas_call(
        functools.partial(_mlstm_kernel, reverse=reverse),
        grid_spec=grid_spec,
        out_shape=jax.ShapeDtypeStruct((t, D_MODEL), BF16),
        compiler_params=_cparams(("arbitrary",)),
        name="mlstm_bwd" if reverse else "mlstm_fwd",
    )(reset, order, *args)


def _outproj_kernel(m_ref, xp_ref, xs_ref, w_ref, g_ref, wr_ref, br_ref, x1_ref, hn_ref, lg_ref,
                    *, n_prompt_tiles):
    i = pl.program_id(0)
    y = jnp.dot(m_ref[...], w_ref[...], preferred_element_type=F32)

    def finish(x):
        x1 = x + y
        x1_ref[...] = x1
        hn = _rms(x1, g_ref[...])
        hn_ref[...] = hn
        lg_ref[...] = jnp.dot(hn, wr_ref[...], precision=lax.Precision.HIGHEST,
                              preferred_element_type=F32) + br_ref[...]

    @pl.when(i < n_prompt_tiles)
    def _():
        finish(xp_ref[...])

    @pl.when(i >= n_prompt_tiles)
    def _():
        finish(xs_ref[...])


def _outproj(merged, xp, xs, w, g, wr, br, tm):
    t = merged.shape[0]
    npt = xp.shape[0] // tm
    full = lambda shape: pl.BlockSpec(shape, lambda i: (0, 0))
    return pl.pallas_call(
        functools.partial(_outproj_kernel, n_prompt_tiles=npt),
        grid=(t // tm,),
        in_specs=[
            pl.BlockSpec((tm, D_MODEL), lambda i: (i, 0)),
            pl.BlockSpec((tm, D_MODEL), lambda i: (jnp.minimum(i, npt - 1), 0)),
            pl.BlockSpec((tm, D_MODEL), lambda i: (jnp.maximum(i - npt, 0), 0)),
            full((D_MODEL, D_MODEL)), full((1, D_MODEL)), full((D_MODEL, LANES)), full((1, LANES)),
        ],
        out_specs=[
            pl.BlockSpec((tm, D_MODEL), lambda i: (i, 0)),
            pl.BlockSpec((tm, D_MODEL), lambda i: (i, 0)),
            pl.BlockSpec((tm, LANES), lambda i: (i, 0)),
        ],
        out_shape=[
            jax.ShapeDtypeStruct((t, D_MODEL), F32),
            jax.ShapeDtypeStruct((t, D_MODEL), F32),
            jax.ShapeDtypeStruct((t, LANES), F32),
        ],
        compiler_params=_cparams(("arbitrary",)),
        name="outproj",
    )(merged, xp, xs, w, g, wr, br)


R_W0, R_W1, R_E0, R_E1, R_RANK0, R_RANK1 = range(6)


def _route_kernel(lg_ref, rt_ref, cnt_ref, carry_ref, *, rows):
    i = pl.program_id(0)

    @pl.when(i == 0)
    def _():
        carry_ref[...] = jnp.zeros_like(carry_ref)

    lg = lg_ref[...]
    lane = lax.broadcasted_iota(jnp.int32, lg.shape, 1).astype(F32)
    big = np.float32(LANES)
    neg = -jnp.inf
    rmax = lambda a: jnp.max(a, axis=-1, keepdims=True)
    rmin = lambda a: jnp.min(a, axis=-1, keepdims=True)

    gmask = lane < N_GROUPS
    gl = jnp.where(gmask, lg, neg)
    gmax = rmax(gl)
    gsel = rmin(jnp.where(gl == gmax, lane, big))
    p_g = 1.0 / jnp.sum(jnp.where(gmask, jnp.exp(gl - gmax), 0.0), axis=-1, keepdims=True)

    lo = N_GROUPS + EXPERTS_PER_GROUP * gsel
    emask = jnp.logical_and(lane >= lo, lane < lo + EXPERTS_PER_GROUP)
    el = jnp.where(emask, lg, neg)
    v0 = rmax(el)
    i0 = rmin(jnp.where(el == v0, lane, big))
    el1 = jnp.where(lane == i0, neg, el)
    v1 = rmax(el1)
    i1 = rmin(jnp.where(el1 == v1, lane, big))
    ex = jnp.exp(v1 - v0)
    w0 = p_g / (1.0 + ex)
    w1 = p_g * ex / (1.0 + ex)
    e0 = i0 - N_GROUPS
    e1 = i1 - N_GROUPS

    oh0 = lane == e0
    oh1 = lane == e1
    onehot = jnp.logical_or(oh0, oh1)
    r_i = lax.broadcasted_iota(jnp.int32, (rows, rows), 0)
    c_i = lax.broadcasted_iota(jnp.int32, (rows, rows), 1)
    strict = (c_i < r_i).astype(BF16)
    before = jnp.dot(strict, onehot.astype(BF16), preferred_element_type=F32) + carry_ref[0:1, :]
    rank0 = jnp.sum(jnp.where(oh0, before, 0.0), axis=-1, keepdims=True)
    rank1 = jnp.sum(jnp.where(oh1, before, 0.0), axis=-1, keepdims=True)
    total = carry_ref[0:1, :] + jnp.sum(onehot.astype(F32), axis=0, keepdims=True)
    carry_ref[...] = jnp.broadcast_to(total, carry_ref.shape)
    cnt_ref[...] = jnp.broadcast_to(total, cnt_ref.shape)

    rec = jnp.zeros(lg.shape, F32)
    for idx, val in ((R_W0, w0), (R_W1, w1), (R_E0, e0), (R_E1, e1), (R_RANK0, rank0), (R_RANK1, rank1)):
        rec = jnp.where(lane == idx, val, rec)
    rt_ref[...] = rec


def _route(logits, rows):
    t = logits.shape[0]
    return pl.pallas_call(
        functools.partial(_route_kernel, rows=rows),
        grid=(t // rows,),
        in_specs=[pl.BlockSpec((rows, LANES), lambda i: (i, 0))],
        out_specs=[pl.BlockSpec((rows, LANES), lambda i: (i, 0)),
                   pl.BlockSpec((8, LANES), lambda i: (0, 0))],
        out_shape=[jax.ShapeDtypeStruct((t, LANES), F32),
                   jax.ShapeDtypeStruct((8, LANES), F32)],
        scratch_shapes=[pltpu.VMEM((8, LANES), F32)],
        compiler_params=_cparams(("arbitrary",)),
        name="route",
    )(logits)


def _experts_kernel(be_ref, nused_ref, nvalid_ref, src_ref, nsrc_ref, dst_ref, x_hbm, rw_ref, wg_ref, wu_ref,
                    wd_ref, out_hbm, xbuf, obuf, gsem, ssem, *, bm):
    del be_ref
    b = pl.program_id(0)
    n_used = nused_ref[0]
    slot = lax.rem(b, 2)

    def start_gather(idx_ref, sl):
        def body(r, _):
            tok = idx_ref[0, 0, r]
            pltpu.make_async_copy(x_hbm.at[pl.ds(tok, 1), :], xbuf.at[sl, pl.ds(r, 1), :], gsem.at[sl]).start()
            return 0
        lax.fori_loop(0, bm, body, 0, unroll=8)

    def wait_rows(sem, buf, sl, n):
        n_al = pl.multiple_of((n // SUBLANES) * SUBLANES, SUBLANES)

        @pl.when(n_al > 0)
        def _():
            pltpu.make_async_copy(buf.at[sl, pl.ds(0, n_al), :], buf.at[sl, pl.ds(0, n_al), :],
                                  sem.at[sl]).wait()

        def one(r, _):
            pltpu.make_async_copy(buf.at[sl, pl.ds(r, 1), :], buf.at[sl, pl.ds(r, 1), :], sem.at[sl]).wait()
            return 0
        lax.fori_loop(n_al, n, one, 0)

    @pl.when(b == 0)
    def _():
        start_gather(src_ref, 0)

    @pl.when(b < n_used)
    def _():
        @pl.when(b + 1 < n_used)
        def _():
            start_gather(nsrc_ref, 1 - slot)

        pltpu.make_async_copy(xbuf.at[slot], xbuf.at[slot], gsem.at[slot]).wait()
        xb = xbuf[slot].astype(BF16)
        hg = jnp.dot(xb, wg_ref[0], preferred_element_type=F32)
        hu = jnp.dot(xb, wu_ref[0], preferred_element_type=F32)
        hmid = (hg * _sigmoid(hg) * hu).astype(BF16)
        o = jnp.dot(hmid, wd_ref[0], preferred_element_type=F32) * rw_ref[...]

        @pl.when(b >= 2)
        def _():
            wait_rows(ssem, obuf, slot, nvalid_ref[jnp.maximum(b - 2, 0)])

        obuf[slot] = o

        def body(r, _):
            k = dst_ref[0, 0, r]
            tok = dst_ref[0, 1, r]
            pltpu.make_async_copy(obuf.at[slot, pl.ds(r, 1), :], out_hbm.at[k, pl.ds(tok, 1), :],
                                  ssem.at[slot]).start()
            return 0
        lax.fori_loop(0, nvalid_ref[b], body, 0)

    @pl.when(jnp.logical_and(b == n_used - 1, n_used >= 2))
    def _():
        wait_rows(ssem, obuf, 1 - slot, nvalid_ref[jnp.maximum(b - 1, 0)])

    @pl.when(b == n_used - 1)
    def _():
        wait_rows(ssem, obuf, slot, nvalid_ref[b])


def _experts(hn, block_e, n_used, n_valid, src, dst, roww, wg, wu, wd, bm, t):
    nblocks = block_e.shape[0]
    src3 = src.reshape(nblocks, 1, bm)
    grid_spec = pltpu.PrefetchScalarGridSpec(
        num_scalar_prefetch=3,
        grid=(nblocks,),
        in_specs=[
            pl.BlockSpec((1, 1, bm), lambda b, be, nu, nv: (b, 0, 0), memory_space=pltpu.SMEM),
            pl.BlockSpec((1, 1, bm), lambda b, be, nu, nv: (jnp.minimum(b + 1, nblocks - 1), 0, 0),
                         memory_space=pltpu.SMEM),
            pl.BlockSpec((1, 2, bm), lambda b, be, nu, nv: (b, 0, 0), memory_space=pltpu.SMEM),
            pl.BlockSpec(memory_space=pl.ANY),
            pl.BlockSpec((bm, 1), lambda b, be, nu, nv: (b, 0)),
            pl.BlockSpec((1, D_MODEL, D_EXPERT), lambda b, be, nu, nv: (be[b], 0, 0)),
            pl.BlockSpec((1, D_MODEL, D_EXPERT), lambda b, be, nu, nv: (be[b], 0, 0)),
            pl.BlockSpec((1, D_EXPERT, D_MODEL), lambda b, be, nu, nv: (be[b], 0, 0)),
        ],
        out_specs=pl.BlockSpec(memory_space=pl.ANY),
        scratch_shapes=[
            pltpu.VMEM((2, bm, D_MODEL), F32),
            pltpu.VMEM((2, bm, D_MODEL), F32),
            pltpu.SemaphoreType.DMA((2,)),
            pltpu.SemaphoreType.DMA((2,)),
        ],
    )
    return pl.pallas_call(
        functools.partial(_experts_kernel, bm=bm),
        grid_spec=grid_spec,
        out_shape=jax.ShapeDtypeStruct((2, t, D_MODEL), F32),
        compiler_params=_cparams(("arbitrary",)),
        name="experts",
    )(block_e, n_used, n_valid, src3, src3, dst, hn, roww, wg, wu, wd)


def _combine_kernel(x1_ref, o0_ref, o1_ref, g_ref, yp_ref, ys_ref, *, n_prompt_tiles):
    i = pl.program_id(0)
    y = _rms(x1_ref[...] + o0_ref[0] + o1_ref[0], g_ref[...])

    @pl.when(i < n_prompt_tiles)
    def _():
        yp_ref[...] = y

    @pl.when(i >= n_prompt_tiles)
    def _():
        ys_ref[...] = y


def _combine(x1, eo, g, tp, ts, tm):
    t = tp + ts
    npt = tp // tm
    return pl.pallas_call(
        functools.partial(_combine_kernel, n_prompt_tiles=npt),
        grid=(t // tm,),
        in_specs=[
            pl.BlockSpec((tm, D_MODEL), lambda i: (i, 0)),
            pl.BlockSpec((1, tm, D_MODEL), lambda i: (0, i, 0)),
            pl.BlockSpec((1, tm, D_MODEL), lambda i: (1, i, 0)),
            pl.BlockSpec((1, D_MODEL), lambda i: (0, 0)),
        ],
        out_specs=[
            pl.BlockSpec((tm, D_MODEL), lambda i: (jnp.minimum(i, npt - 1), 0)),
            pl.BlockSpec((tm, D_MODEL), lambda i: (jnp.maximum(i - npt, 0), 0)),
        ],
        out_shape=[jax.ShapeDtypeStruct((tp, D_MODEL), F32),
                   jax.ShapeDtypeStruct((ts, D_MODEL), F32)],
        compiler_params=_cparams(("arbitrary",)),
        name="combine",
    )(x1, eo, eo, g)


def _tile(n, pref):
    t = min(pref, n)
    assert n % t == 0, (n, t)
    return t


def _layer(xp, xs, seq_lens, norm_mix_g, norm_ffn_g, norm_final_g, w_in, b_cg, gmlp_ln_g, gmlp_w_s,
           gmlp_b_s, conv_w, head_g, w_out, w_rg, b_rg, w_re, b_re, w_eg, w_eu, w_ed):
    tp, ts = xp.shape[0], xs.shape[0]
    t = tp + ts
    row2 = lambda a: a.reshape(1, -1).astype(F32)

    w_main = jnp.concatenate([w_in[:, :OFF_CG], w_in[:, OFF_MERGE:]], axis=1).astype(BF16)
    w_cg = jnp.pad(w_in[:, OFF_CG:OFF_MERGE], ((0, 0), (0, LANES - N_CELL_GATES))).astype(BF16)
    bias_cg = jnp.pad(b_cg.astype(F32), (0, LANES - N_CELL_GATES)).reshape(1, LANES)
    bst = jnp.pad(gmlp_b_s.astype(F32).T, ((0, 0), (0, LANES - G_GROUPS)))
    conv_w8 = jnp.pad(conv_w.astype(F32), ((0, 3), (0, 0)))
    w_router = jnp.pad(jnp.concatenate([w_rg, w_re], axis=1).astype(F32),
                       ((0, 0), (0, LANES - N_GROUPS - N_EXPERTS)))
    b_router = jnp.pad(jnp.concatenate([b_rg, b_re]).astype(F32),
                       (0, LANES - N_GROUPS - N_EXPERTS)).reshape(1, LANES)

    tm = _tile(np.gcd(tp, ts), 1024)
    conv_rows = _tile(np.gcd.reduce(seq_lens), 512)
    starts = np.cumsum([0] + list(seq_lens))
    n_ct = t // conv_rows
    flags = np.zeros((n_ct, 2), np.int32)
    for s0, s1 in zip(starts[:-1], starts[1:]):
        flags[s0 // conv_rows, 0] = 1
        flags[s1 // conv_rows - 1, 1] = 1
    nchunks = t // CHUNK
    first = np.zeros((nchunks,), np.int32)
    last = np.zeros((nchunks,), np.int32)
    for s0, s1 in zip(starts[:-1], starts[1:]):
        first[s0 // CHUNK] = 1
        last[s1 // CHUNK - 1] = 1
    fwd_order = np.arange(nchunks, dtype=np.int32)
    bwd_order = fwd_order[::-1].copy()

    z, cg = _proj(xp, xs, row2(norm_mix_g), w_main, w_cg, tm, _tile(Z_COLS, 1024))
    ga = _gmlp(z, row2(gmlp_ln_g), gmlp_w_s.astype(BF16), bst, _tile(np.gcd(tp, ts), 512))
    qk = _conv(z, conv_w8, jnp.asarray(flags), conv_rows, 1024)
    hf = _mlstm(qk, z, cg, bias_cg, jnp.asarray(first), jnp.asarray(fwd_order), reverse=False)
    merged = _mlstm(qk, z, cg, bias_cg, jnp.asarray(last[::-1].copy()), jnp.asarray(bwd_order), reverse=True,
                    extras=(hf, ga, head_g.astype(F32)))
    x1, hn, logits = _outproj(merged, xp, xs, w_out.astype(BF16), row2(norm_ffn_g), w_router, b_router,
                              _tile(np.gcd(tp, ts), 512))

    route, counts = _route(logits, _tile(t, 512))
    bm = 256
    n_assign = 2 * t
    nblocks = n_assign // bm + N_EXPERTS
    counts = counts[0, :N_EXPERTS].astype(jnp.int32)
    padded = ((counts + bm - 1) // bm) * bm
    pad_end = jnp.cumsum(padded)
    pad_start = pad_end - padded
    e_id = route[:, R_E0:R_E1 + 1].astype(jnp.int32)
    rank = route[:, R_RANK0:R_RANK1 + 1].astype(jnp.int32)
    dest = (pad_start[e_id] + rank).reshape(-1)
    tok = jnp.repeat(jnp.arange(t, dtype=jnp.int32), 2)
    kk = jnp.tile(jnp.arange(2, dtype=jnp.int32), t)
    n_rows = nblocks * bm
    src = jnp.zeros((n_rows,), jnp.int32).at[dest].set(tok)
    dst_k = jnp.zeros((n_rows,), jnp.int32).at[dest].set(kk)
    dst_t = jnp.zeros((n_rows,), jnp.int32).at[dest].set(tok)
    dst = jnp.stack([dst_k.reshape(nblocks, bm), dst_t.reshape(nblocks, bm)], axis=1)
    roww = jnp.zeros((n_rows,), F32).at[dest].set(route[:, R_W0:R_W1 + 1].reshape(-1)).reshape(n_rows, 1)
    block_start = jnp.arange(nblocks, dtype=jnp.int32) * bm
    block_e = jnp.minimum(jnp.searchsorted(pad_end, block_start, side="right"), N_EXPERTS - 1).astype(jnp.int32)
    n_used = (pad_end[-1] // bm).astype(jnp.int32).reshape(1)
    used = block_start < pad_end[-1]
    n_valid = jnp.where(used, jnp.clip(pad_start[block_e] + counts[block_e] - block_start, 0, bm), 0)
    block_e = jnp.where(used, block_e, block_e[jnp.maximum(n_used[0] - 1, 0)])
    eo = _experts(hn, block_e, n_used, n_valid.astype(jnp.int32), src, dst, roww, w_eg.astype(BF16),
                  w_eu.astype(BF16), w_ed.astype(BF16), bm, t)
    return _combine(x1, eo, row2(norm_final_g), tp, ts, _tile(np.gcd(tp, ts), 512))


def kernel(x_prompt, x_sample, norm_mix_g, norm_ffn_g, norm_final_g, w_in, b_cell_gates, gmlp_ln_g, gmlp_w_s,
           gmlp_b_s, mlstm_conv_w, mlstm_head_g, w_out, w_router_group, b_router_group, w_router_expert,
           b_router_expert, w_exp_gate, w_exp_up, w_exp_down):
    bp, sp, d = x_prompt.shape
    bs, ss, _ = x_sample.shape
    seq_lens = [sp] * bp + [ss] * bs
    yp, ys = _layer(
        x_prompt.reshape(bp * sp, d), x_sample.reshape(bs * ss, d), seq_lens,
        norm_mix_g[0], norm_ffn_g[0], norm_final_g, w_in[0], b_cell_gates[0], gmlp_ln_g[0], gmlp_w_s[0],
        gmlp_b_s[0], mlstm_conv_w[0], mlstm_head_g[0], w_out[0], w_router_group[0], b_router_group[0],
        w_router_expert[0], b_router_expert[0], w_exp_gate[0], w_exp_up[0], w_exp_down[0])
    return yp.reshape(bp, sp, d), ys.reshape(bs, ss, d)
```

```python
import functools

import jax
import jax.numpy as jnp
import numpy as np
from jax import lax
from jax.experimental import pallas as pl
from jax.experimental.pallas import tpu as pltpu

F32 = jnp.float32
BF16 = jnp.bfloat16
U32 = jnp.uint32
HI_HALF = np.uint32(0xFFFF0000)

D_MODEL = 2048
HALF_D = D_MODEL // 2
CHUNK = 128
G_GROUPS = 8
G_GDIM = D_MODEL // G_GROUPS
M_HEADS = 8
M_HDIM = D_MODEL // M_HEADS
CONV_W = 5
N_GROUPS = 4
EXPERTS_PER_GROUP = 8
N_EXPERTS = N_GROUPS * EXPERTS_PER_GROUP
D_EXPERT = 1024
EPS = 1e-6
LANES = 128
SUBLANES = 8
N_CELL_GATES = 4 * M_HEADS
OFF_CG = 6 * D_MODEL
OFF_MERGE = OFF_CG + N_CELL_GATES

ZB_GU, ZB_GV, ZB_Q, ZB_K, ZB_V, ZB_O, ZB_MA, ZB_MB = range(8)
Z_COLS = 8 * D_MODEL

V_AUG = M_HDIM + LANES
VMEM_LIMIT = 56 * 1024 * 1024
EXPERT_BLOCK = 512
DISPATCH_ROWS = 1024
COMBINE_ROWS = 256


def _cparams(sem):
    return pltpu.CompilerParams(dimension_semantics=sem, vmem_limit_bytes=VMEM_LIMIT)


def _rms(x, g):
    ms = jnp.mean(x * x, axis=-1, keepdims=True)
    return x * lax.rsqrt(ms + EPS) * g


def _sigmoid(x):
    return 1.0 / (1.0 + jnp.exp(-x))


def _gelu_tanh(x):
    c = np.float32(np.sqrt(2.0 / np.pi))
    return 0.5 * x * (1.0 + jnp.tanh(c * (x + np.float32(0.044715) * (x * x * x))))


def _proj_kernel(xp_ref, xs_ref, g_ref, w_ref, wcg_ref, z_ref, cg_ref, h_ref, *, n_prompt_tiles, tn):
    i = pl.program_id(0)
    j = pl.program_id(1)

    @pl.when(j == 0)
    def _():
        @pl.when(i < n_prompt_tiles)
        def _():
            h_ref[...] = _rms(xp_ref[...], g_ref[...]).astype(BF16)

        @pl.when(i >= n_prompt_tiles)
        def _():
            h_ref[...] = _rms(xs_ref[...], g_ref[...]).astype(BF16)

        cg_ref[...] = jnp.dot(h_ref[...], wcg_ref[...], preferred_element_type=F32)

    acc = jnp.dot(h_ref[...], w_ref[...], preferred_element_type=F32)
    n_gelu = 2 * D_MODEL // tn
    n_plain = 3 * D_MODEL // tn

    @pl.when(j < n_gelu)
    def _():
        z_ref[...] = _gelu_tanh(acc).astype(BF16)

    @pl.when(jnp.logical_and(j >= n_gelu, j < n_gelu + n_plain))
    def _():
        z_ref[...] = acc.astype(BF16)

    @pl.when(j >= n_gelu + n_plain)
    def _():
        z_ref[...] = _sigmoid(acc).astype(BF16)


def _proj(xp, xs, g, w, wcg, tm, tn):
    tp, ts = xp.shape[0], xs.shape[0]
    t = tp + ts
    npt, nst = tp // tm, ts // tm
    kern = functools.partial(_proj_kernel, n_prompt_tiles=npt, tn=tn)
    return pl.pallas_call(
        kern,
        grid=(npt + nst, Z_COLS // tn),
        in_specs=[
            pl.BlockSpec((tm, D_MODEL), lambda i, j: (jnp.minimum(i, npt - 1), 0)),
            pl.BlockSpec((tm, D_MODEL), lambda i, j: (jnp.maximum(i - npt, 0), 0)),
            pl.BlockSpec((1, D_MODEL), lambda i, j: (0, 0)),
            pl.BlockSpec((D_MODEL, tn), lambda i, j: (0, j)),
            pl.BlockSpec((D_MODEL, LANES), lambda i, j: (0, 0)),
        ],
        out_specs=[
            pl.BlockSpec((tm, tn), lambda i, j: (i, j)),
            pl.BlockSpec((tm, LANES), lambda i, j: (i, 0)),
        ],
        out_shape=[
            jax.ShapeDtypeStruct((t, Z_COLS), BF16),
            jax.ShapeDtypeStruct((t, LANES), F32),
        ],
        scratch_shapes=[pltpu.VMEM((tm, D_MODEL), BF16)],
        compiler_params=_cparams(("arbitrary", "arbitrary")),
        name="proj",
    )(xp, xs, g, w, wcg)


def _gmlp_kernel(gu_ref, gv_ref, ma_ref, lng_ref, ws_ref, bst_ref, out_ref, vn_ref, *, rows):
    v = gv_ref[...].astype(F32)
    mu = jnp.mean(v, axis=-1, keepdims=True)
    vc = v - mu
    var = jnp.mean(vc * vc, axis=-1, keepdims=True)
    vn_ref[...] = (vc * lax.rsqrt(var + EPS) * lng_ref[...]).astype(BF16)
    for c in range(rows // CHUNK):
        rs = slice(c * CHUNK, (c + 1) * CHUNK)
        for g in range(G_GROUPS):
            cs = slice(g * G_GDIM, (g + 1) * G_GDIM)
            s = jnp.dot(ws_ref[g], vn_ref[rs, cs], preferred_element_type=F32) + bst_ref[:, g:g + 1]
            a = gu_ref[rs, cs].astype(F32) * s
            out_ref[rs, cs] = (ma_ref[rs, cs].astype(F32) * a).astype(BF16)


def _gmlp(z, lng, ws, bst, rows):
    t = z.shape[0]
    zspec = lambda b: pl.BlockSpec((rows, D_MODEL), lambda i, b=b: (i, b))
    return pl.pallas_call(
        functools.partial(_gmlp_kernel, rows=rows),
        grid=(t // rows,),
        in_specs=[
            zspec(ZB_GU), zspec(ZB_GV), zspec(ZB_MA),
            pl.BlockSpec((1, D_MODEL), lambda i: (0, 0)),
            pl.BlockSpec((G_GROUPS, CHUNK, CHUNK), lambda i: (0, 0, 0)),
            pl.BlockSpec((CHUNK, LANES), lambda i: (0, 0)),
        ],
        out_specs=pl.BlockSpec((rows, D_MODEL), lambda i: (i, 0)),
        out_shape=jax.ShapeDtypeStruct((t, D_MODEL), BF16),
        scratch_shapes=[pltpu.VMEM((rows, D_MODEL), BF16)],
        compiler_params=_cparams(("arbitrary",)),
        name="gmlp",
    )(z, z, z, lng, ws, bst)


HALO = 16


def _conv_kernel(flags_ref, prev_ref, cur_ref, next_ref, w_ref, q_ref, kt_ref, *, rows, n_q_blocks):
    i = pl.program_id(0)
    j = pl.program_id(1)
    x = cur_ref[...].astype(F32)
    keep_prev = (flags_ref[i, 0] == 0).astype(F32)
    keep_next = (flags_ref[i, 1] == 0).astype(F32)
    p = prev_ref[...].astype(F32) * keep_prev
    n = next_ref[...].astype(F32) * keep_next
    row = lax.broadcasted_iota(jnp.int32, x.shape, 0)
    w = w_ref[...]
    half = CONV_W // 2
    acc = x * w[half:half + 1, :]
    for d in range(1, half + 1):
        xm = pltpu.roll(x, d, 0)
        for r in range(d):
            xm = jnp.where(row == r, p[HALO - d + r:HALO - d + r + 1, :], xm)
        acc = acc + xm * w[half - d:half - d + 1, :]
        xp = pltpu.roll(x, rows - d, 0)
        for r in range(d):
            xp = jnp.where(row == rows - d + r, n[r:r + 1, :], xp)
        acc = acc + xp * w[half + d:half + d + 1, :]
    y = acc * _sigmoid(acc)

    @pl.when(j < n_q_blocks)
    def _():
        q_ref[...] = y.astype(BF16)

    @pl.when(j >= n_q_blocks)
    def _():
        kt_ref[...] = (y * np.float32(M_HDIM ** -0.5)).T.astype(BF16)


def _conv(z, w, flags, rows, cols):
    t = z.shape[0]
    nqb = D_MODEL // cols
    cb0 = ZB_Q * D_MODEL // cols
    hb = rows // HALO
    nhb = t // HALO
    kern = functools.partial(_conv_kernel, rows=rows, n_q_blocks=nqb)
    grid_spec = pltpu.PrefetchScalarGridSpec(
        num_scalar_prefetch=1,
        grid=(t // rows, 2 * nqb),
        in_specs=[
            pl.BlockSpec((HALO, cols), lambda i, j, f: (jnp.maximum(i * hb - 1, 0), cb0 + j)),
            pl.BlockSpec((rows, cols), lambda i, j, f: (i, cb0 + j)),
            pl.BlockSpec((HALO, cols), lambda i, j, f: (jnp.minimum((i + 1) * hb, nhb - 1), cb0 + j)),
            pl.BlockSpec((CONV_W + 3, cols), lambda i, j, f: (0, j)),
        ],
        out_specs=[
            pl.BlockSpec((rows, cols), lambda i, j, f: (i, jnp.minimum(j, nqb - 1))),
            pl.BlockSpec((cols, rows), lambda i, j, f: (jnp.maximum(j - nqb, 0), i)),
        ],
    )
    return pl.pallas_call(
        kern,
        grid_spec=grid_spec,
        out_shape=[jax.ShapeDtypeStruct((t, D_MODEL), BF16), jax.ShapeDtypeStruct((D_MODEL, t), BF16)],
        compiler_params=_cparams(("arbitrary", "arbitrary")),
        name="conv",
    )(flags, z, z, z, w)


def _log_sigmoid(x):
    return jnp.minimum(x, 0.0) - jnp.log1p(jnp.exp(-jnp.abs(x)))


def _running_max_lanes(x, reverse):
    n = x.shape[1]
    lane = lax.broadcasted_iota(jnp.int32, x.shape, 1)
    shift = 1
    while shift < n:
        if reverse:
            moved = jnp.where(lane < n - shift, pltpu.roll(x, n - shift, 1), -jnp.inf)
        else:
            moved = jnp.where(lane >= shift, pltpu.roll(x, shift, 1), -jnp.inf)
        x = jnp.maximum(x, moved)
        shift *= 2
    return x


def _mlstm_kernel(reset_ref, order_ref, *refs, reverse):
    del order_ref
    if reverse:
        (q_ref, kt_ref, v_ref, cg_ref, bias_ref, hf_ref, o_ref, mb_ref, ga_ref, hg_ref,
         out_ref, c_ref, m_ref) = refs
    else:
        q_ref, kt_ref, v_ref, cg_ref, bias_ref, out_ref, c_ref, m_ref = refs
    step = pl.program_id(0)

    @pl.when(reset_ref[step] == 1)
    def _():
        c_ref[...] = jnp.zeros_like(c_ref)
        m_ref[...] = jnp.zeros_like(m_ref)

    L = CHUNK
    gates_t = (cg_ref[...] + bias_ref[...]).T
    lf_t = _log_sigmoid(gates_t)
    r_i = lax.broadcasted_iota(jnp.int32, (L, L), 0)
    c_i = lax.broadcasted_iota(jnp.int32, (L, L), 1)
    if reverse:
        valid = c_i >= r_i
        i_off, f_off, last = 2 * M_HEADS, 3 * M_HEADS, 0
    else:
        valid = c_i <= r_i
        i_off, f_off, last = 0, M_HEADS, L - 1
    tri = valid.astype(F32)
    f_rows = lax.dot_general(lf_t, tri, (((1,), (1,)), ((), ())), precision=lax.Precision.HIGHEST,
                             preferred_element_type=F32)[f_off:f_off + M_HEADS, :]
    b_rows = gates_t[i_off:i_off + M_HEADS, :] - f_rows
    cm_rows = _running_max_lanes(b_rows, reverse)
    cols = jnp.concatenate([cm_rows, f_rows, jnp.zeros((L - 2 * M_HEADS, L), F32)], axis=0).T
    lane = lax.broadcasted_iota(jnp.int32, (L, LANES), 1)
    ones_blk = (lane == 0).astype(BF16)

    for h in range(M_HEADS):
        hs = slice(h * M_HDIM, (h + 1) * M_HDIM)
        q = q_ref[:, hs]
        kt = kt_ref[hs, :]
        v_aug = jnp.concatenate([v_ref[:, hs], ones_blk], axis=1)
        brow = b_rows[h:h + 1, :]
        f_col = cols[:, M_HEADS + h:M_HEADS + h + 1]
        f_last = f_rows[h:h + 1, last:last + 1]
        m_old = m_ref[h][:, 0:1]

        u = jnp.maximum(m_old, cols[:, h:h + 1])
        s = jnp.dot(q, kt, preferred_element_type=F32)
        p = (s * jnp.exp(jnp.where(valid, brow - u, -jnp.inf))).astype(BF16)
        w_inter = jnp.exp(m_old - u)
        c_old = c_ref[h]
        qc = jnp.dot(q, c_old.astype(BF16), preferred_element_type=F32)
        nd = w_inter * qc + jnp.dot(p, v_aug, preferred_element_type=F32)
        den = nd[:, M_HDIM:M_HDIM + 1]
        hout = nd[:, :M_HDIM] * (1.0 / jnp.maximum(jnp.abs(den), jnp.exp(-(f_col + u))))

        g_row = f_last + brow
        m_new = jnp.maximum(f_last + m_old, jnp.max(g_row, axis=-1, keepdims=True))
        a = jnp.exp(f_last + m_old - m_new)
        kwt = (kt.astype(F32) * jnp.exp(g_row - m_new)).astype(BF16)
        c_ref[h] = a * c_old + jnp.dot(kwt, v_aug, preferred_element_type=F32)
        m_ref[h] = jnp.broadcast_to(m_new, (1, LANES))

        if reverse:
            hsum = hf_ref[:, hs].astype(F32) + hout
            hn = _rms(hsum, hg_ref[h:h + 1, :])
            b = hn * o_ref[:, hs].astype(F32)
            merged = ga_ref[:, hs].astype(F32) + mb_ref[:, hs].astype(F32) * b
            out_ref[:, hs] = merged.astype(BF16)
        else:
            out_ref[:, hs] = hout.astype(BF16)


def _mlstm(qc, kt, z, cg, bias, reset, order, reverse, extras=None):
    t = z.shape[0]
    nchunks = t // CHUNK
    row = lambda b: pl.BlockSpec((CHUNK, D_MODEL), lambda i, rs, od, b=b: (od[i], b))
    in_specs = [row(0),
                pl.BlockSpec((D_MODEL, CHUNK), lambda i, rs, od: (0, od[i])),
                row(ZB_V),
                pl.BlockSpec((CHUNK, LANES), lambda i, rs, od: (od[i], 0)),
                pl.BlockSpec((1, LANES), lambda i, rs, od: (0, 0))]
    args = [qc, kt, z, cg, bias]
    if reverse:
        hf, ga, hg = extras
        in_specs += [row(0), row(ZB_O), row(ZB_MB), row(0),
                     pl.BlockSpec((M_HEADS, M_HDIM), lambda i, rs, od: (0, 0))]
        args += [hf, z, z, ga, hg]
    grid_spec = pltpu.PrefetchScalarGridSpec(
        num_scalar_prefetch=2,
        grid=(nchunks,),
        in_specs=in_specs,
        out_specs=row(0),
        scratch_shapes=[pltpu.VMEM((M_HEADS, M_HDIM, V_AUG), F32),
                        pltpu.VMEM((M_HEADS, 1, LANES), F32)],
    )
    return pl.pallas_call(
        functools.partial(_mlstm_kernel, reverse=reverse),
        grid_spec=grid_spec,
        out_shape=jax.ShapeDtypeStruct((t, D_MODEL), BF16),
        compiler_params=_cparams(("arbitrary",)),
        name="mlstm_bwd" if reverse else "mlstm_fwd",
    )(reset, order, *args)


def _pack_bf16_pair(lo, hi):
    lo_bits = pltpu.bitcast(lo.astype(F32), U32) >> 16
    hi_bits = pltpu.bitcast(hi.astype(F32), U32) & HI_HALF
    return hi_bits | lo_bits


def _unpack_bf16_pair(p):
    lo = pltpu.bitcast(p << 16, F32)
    hi = pltpu.bitcast(p & HI_HALF, F32)
    return lo.astype(BF16), hi.astype(BF16)


def _outproj_kernel(m_ref, xp_ref, xs_ref, w_ref, g_ref, wr_hi_ref, wr_lo_ref, br_ref, x1_ref, hp_ref, lg_ref,
                    *, n_prompt_tiles):
    i = pl.program_id(0)
    y = jnp.dot(m_ref[...], w_ref[...], preferred_element_type=F32)

    def finish(x):
        x1 = x + y
        x1_ref[...] = x1
        hn = _rms(x1, g_ref[...])
        h_hi = hn.astype(BF16)
        h_lo = (hn - h_hi.astype(F32)).astype(BF16)
        hp_ref[...] = _pack_bf16_pair(h_hi[:, :HALF_D], h_hi[:, HALF_D:])
        dot = functools.partial(jnp.dot, preferred_element_type=F32)
        lg_ref[...] = (dot(h_hi, wr_hi_ref[...]) + dot(h_hi, wr_lo_ref[...]) + dot(h_lo, wr_hi_ref[...])
                       + br_ref[...])

    @pl.when(i < n_prompt_tiles)
    def _():
        finish(xp_ref[...])

    @pl.when(i >= n_prompt_tiles)
    def _():
        finish(xs_ref[...])


def _outproj(merged, xp, xs, w, g, wr, br, tm):
    t = merged.shape[0]
    npt = xp.shape[0] // tm
    full = lambda shape: pl.BlockSpec(shape, lambda i: (0, 0))
    wr_hi = wr.astype(BF16)
    wr_lo = (wr - wr_hi.astype(F32)).astype(BF16)
    return pl.pallas_call(
        functools.partial(_outproj_kernel, n_prompt_tiles=npt),
        grid=(t // tm,),
        in_specs=[
            pl.BlockSpec((tm, D_MODEL), lambda i: (i, 0)),
            pl.BlockSpec((tm, D_MODEL), lambda i: (jnp.minimum(i, npt - 1), 0)),
            pl.BlockSpec((tm, D_MODEL), lambda i: (jnp.maximum(i - npt, 0), 0)),
            full((D_MODEL, D_MODEL)), full((1, D_MODEL)), full((D_MODEL, LANES)), full((D_MODEL, LANES)),
            full((1, LANES)),
        ],
        out_specs=[
            pl.BlockSpec((tm, D_MODEL), lambda i: (i, 0)),
            pl.BlockSpec((tm, HALF_D), lambda i: (i, 0)),
            pl.BlockSpec((tm, LANES), lambda i: (i, 0)),
        ],
        out_shape=[
            jax.ShapeDtypeStruct((t, D_MODEL), F32),
            jax.ShapeDtypeStruct((t, HALF_D), U32),
            jax.ShapeDtypeStruct((t, LANES), F32),
        ],
        compiler_params=_cparams(("arbitrary",)),
        name="outproj",
    )(merged, xp, xs, w, g, wr_hi, wr_lo, br)


R_W0, R_W1, R_E0, R_E1, R_RANK0, R_RANK1 = range(6)


def _route_kernel(lg_ref, rt_ref, cnt_ref, carry_ref, *, rows):
    i = pl.program_id(0)

    @pl.when(i == 0)
    def _():
        carry_ref[...] = jnp.zeros_like(carry_ref)

    lg = lg_ref[...]
    lane = lax.broadcasted_iota(jnp.int32, lg.shape, 1).astype(F32)
    big = np.float32(LANES)
    neg = -jnp.inf
    rmax = lambda a: jnp.max(a, axis=-1, keepdims=True)
    rmin = lambda a: jnp.min(a, axis=-1, keepdims=True)

    gmask = lane < N_GROUPS
    gl = jnp.where(gmask, lg, neg)
    gmax = rmax(gl)
    gsel = rmin(jnp.where(gl == gmax, lane, big))
    p_g = 1.0 / jnp.sum(jnp.where(gmask, jnp.exp(gl - gmax), 0.0), axis=-1, keepdims=True)

    lo = N_GROUPS + EXPERTS_PER_GROUP * gsel
    emask = jnp.logical_and(lane >= lo, lane < lo + EXPERTS_PER_GROUP)
    el = jnp.where(emask, lg, neg)
    v0 = rmax(el)
    i0 = rmin(jnp.where(el == v0, lane, big))
    el1 = jnp.where(lane == i0, neg, el)
    v1 = rmax(el1)
    i1 = rmin(jnp.where(el1 == v1, lane, big))
    ex = jnp.exp(v1 - v0)
    w0 = p_g / (1.0 + ex)
    w1 = p_g * ex / (1.0 + ex)
    e0 = i0 - N_GROUPS
    e1 = i1 - N_GROUPS

    oh0 = lane == e0
    oh1 = lane == e1
    onehot = jnp.logical_or(oh0, oh1)
    r_i = lax.broadcasted_iota(jnp.int32, (rows, rows), 0)
    c_i = lax.broadcasted_iota(jnp.int32, (rows, rows), 1)
    strict = (c_i < r_i).astype(BF16)
    before = jnp.dot(strict, onehot.astype(BF16), preferred_element_type=F32) + carry_ref[0:1, :]
    rank0 = jnp.sum(jnp.where(oh0, before, 0.0), axis=-1, keepdims=True)
    rank1 = jnp.sum(jnp.where(oh1, before, 0.0), axis=-1, keepdims=True)
    total = carry_ref[0:1, :] + jnp.sum(onehot.astype(F32), axis=0, keepdims=True)
    carry_ref[...] = jnp.broadcast_to(total, carry_ref.shape)
    cnt_ref[...] = jnp.broadcast_to(total, cnt_ref.shape)

    rec = jnp.zeros(lg.shape, F32)
    for idx, val in ((R_W0, w0), (R_W1, w1), (R_E0, e0), (R_E1, e1), (R_RANK0, rank0), (R_RANK1, rank1)):
        rec = jnp.where(lane == idx, val, rec)
    rt_ref[...] = rec


def _route(logits, rows):
    t = logits.shape[0]
    return pl.pallas_call(
        functools.partial(_route_kernel, rows=rows),
        grid=(t // rows,),
        in_specs=[pl.BlockSpec((rows, LANES), lambda i: (i, 0))],
        out_specs=[pl.BlockSpec((rows, LANES), lambda i: (i, 0)),
                   pl.BlockSpec((8, LANES), lambda i: (0, 0))],
        out_shape=[jax.ShapeDtypeStruct((t, LANES), F32),
                   jax.ShapeDtypeStruct((8, LANES), F32)],
        scratch_shapes=[pltpu.VMEM((8, LANES), F32)],
        compiler_params=_cparams(("arbitrary",)),
        name="route",
    )(logits)


def _dispatch_kernel(zstart_ref, zcount_ref, nused_ref, dest_ref, hp_hbm, xs_hbm, zbuf, sem, zsem,
                     *, tm, bm, nblocks):
    i = pl.program_id(0)

    @pl.when(i == 0)
    def _():
        zbuf[...] = jnp.zeros_like(zbuf)

        def per_expert(e, _):
            start, n = zstart_ref[e], zcount_ref[e]

            def issue(r, _):
                pltpu.make_async_copy(zbuf.at[pl.ds(0, 1), :], xs_hbm.at[pl.ds(start + r, 1), :], zsem).start()
                return 0

            def wait(r, _):
                pltpu.make_async_copy(zbuf.at[pl.ds(0, 1), :], xs_hbm.at[pl.ds(start + r, 1), :], zsem).wait()
                return 0
            lax.fori_loop(0, n, issue, 0)
            lax.fori_loop(0, n, wait, 0)
            return 0
        lax.fori_loop(0, N_EXPERTS, per_expert, 0)

        def tail_copy(b):
            row0 = pl.multiple_of(b * bm, bm)
            return pltpu.make_async_copy(zbuf, xs_hbm.at[pl.ds(row0, bm), :], zsem)

        def tail_issue(b, _):
            tail_copy(b).start()
            return 0

        def tail_wait(b, _):
            tail_copy(b).wait()
            return 0
        lax.fori_loop(nused_ref[0], nblocks, tail_issue, 0)
        lax.fori_loop(nused_ref[0], nblocks, tail_wait, 0)

    base = i * tm

    def body(r, _):
        for k in range(2):
            pltpu.make_async_copy(hp_hbm.at[pl.ds(base + r, 1), :], xs_hbm.at[pl.ds(dest_ref[0, k, r], 1), :],
                                  sem).start()
        return 0
    lax.fori_loop(0, tm, body, 0, unroll=8)
    pltpu.make_async_copy(xs_hbm.at[pl.ds(0, 2 * tm), :], xs_hbm.at[pl.ds(0, 2 * tm), :], sem).wait()


def _dispatch(hp, dest3, zstart, zcount, n_used, bm, nblocks):
    ntiles, _, tm = dest3.shape
    grid_spec = pltpu.PrefetchScalarGridSpec(
        num_scalar_prefetch=3,
        grid=(ntiles,),
        in_specs=[
            pl.BlockSpec((1, 2, tm), lambda i, zs, zc, nu: (i, 0, 0), memory_space=pltpu.SMEM),
            pl.BlockSpec(memory_space=pl.ANY),
        ],
        out_specs=pl.BlockSpec(memory_space=pl.ANY),
        scratch_shapes=[
            pltpu.VMEM((bm, HALF_D), U32),
            pltpu.SemaphoreType.DMA(()),
            pltpu.SemaphoreType.DMA(()),
        ],
    )
    return pl.pallas_call(
        functools.partial(_dispatch_kernel, tm=tm, bm=bm, nblocks=nblocks),
        grid_spec=grid_spec,
        out_shape=jax.ShapeDtypeStruct((nblocks * bm, HALF_D), U32),
        compiler_params=_cparams(("arbitrary",)),
        name="dispatch",
    )(zstart, zcount, n_used, dest3, hp)


def _experts_kernel(be_ref, nused_ref, xs_ref, wg_ref, wu_ref, wd_ref, out_ref):
    del be_ref
    b = pl.program_id(0)

    @pl.when(b < nused_ref[0])
    def _():
        x_lo, x_hi = _unpack_bf16_pair(xs_ref[...])
        xb = jnp.concatenate([x_lo, x_hi], axis=1)
        hg = jnp.dot(xb, wg_ref[0], preferred_element_type=F32)
        hu = jnp.dot(xb, wu_ref[0], preferred_element_type=F32)
        hmid = (hg * _sigmoid(hg) * hu).astype(BF16)
        o = jnp.dot(hmid, wd_ref[0], preferred_element_type=F32).astype(BF16)
        out_ref[...] = _pack_bf16_pair(o[:, :HALF_D], o[:, HALF_D:])

    @pl.when(b >= nused_ref[0])
    def _():
        out_ref[...] = jnp.zeros_like(out_ref)


def _experts(xs, block_e, n_used, wg, wu, wd, bm):
    nblocks = block_e.shape[0]
    wspec = lambda shape: pl.BlockSpec((1,) + shape, lambda b, be, nu: (be[b], 0, 0))
    grid_spec = pltpu.PrefetchScalarGridSpec(
        num_scalar_prefetch=2,
        grid=(nblocks,),
        in_specs=[
            pl.BlockSpec((bm, HALF_D), lambda b, be, nu: (jnp.minimum(b, nu[0] - 1), 0)),
            wspec((D_MODEL, D_EXPERT)), wspec((D_MODEL, D_EXPERT)), wspec((D_EXPERT, D_MODEL)),
        ],
        out_specs=pl.BlockSpec((bm, HALF_D), lambda b, be, nu: (b, 0)),
    )
    return pl.pallas_call(
        _experts_kernel,
        grid_spec=grid_spec,
        out_shape=jax.ShapeDtypeStruct((nblocks * bm, HALF_D), U32),
        compiler_params=_cparams(("arbitrary",)),
        name="experts",
    )(block_e, n_used, xs, wg, wu, wd)


def _combine_kernel(dcur_ref, dnext_ref, x1_ref, rt_ref, g_ref, eo_hbm, yp_ref, ys_ref, gbuf, gsem,
                    *, tm, n_prompt_tiles, n_tiles):
    i = pl.program_id(0)
    slot = lax.rem(i, 2)

    def start_gather(idx_ref, sl):
        def body(r, _):
            for k in range(2):
                pltpu.make_async_copy(eo_hbm.at[pl.ds(idx_ref[0, k, r], 1), :], gbuf.at[sl, k, pl.ds(r, 1), :],
                                      gsem.at[sl]).start()
            return 0
        lax.fori_loop(0, tm, body, 0, unroll=8)

    @pl.when(i == 0)
    def _():
        start_gather(dcur_ref, 0)

    @pl.when(i + 1 < n_tiles)
    def _():
        start_gather(dnext_ref, 1 - slot)

    pltpu.make_async_copy(gbuf.at[slot], gbuf.at[slot], gsem.at[slot]).wait()
    rt = rt_ref[...]
    w0 = rt[:, R_W0:R_W0 + 1]
    w1 = rt[:, R_W1:R_W1 + 1]
    lo0, hi0 = _unpack_bf16_pair(gbuf[slot, 0])
    lo1, hi1 = _unpack_bf16_pair(gbuf[slot, 1])
    y_lo = x1_ref[:, :HALF_D] + w0 * lo0.astype(F32) + w1 * lo1.astype(F32)
    y_hi = x1_ref[:, HALF_D:] + w0 * hi0.astype(F32) + w1 * hi1.astype(F32)
    ms = (jnp.sum(y_lo * y_lo, axis=-1, keepdims=True) + jnp.sum(y_hi * y_hi, axis=-1, keepdims=True)) / D_MODEL
    scale = lax.rsqrt(ms + EPS)

    def store(y_ref):
        y_ref[:, :HALF_D] = y_lo * scale * g_ref[:, :HALF_D]
        y_ref[:, HALF_D:] = y_hi * scale * g_ref[:, HALF_D:]

    @pl.when(i < n_prompt_tiles)
    def _():
        store(yp_ref)

    @pl.when(i >= n_prompt_tiles)
    def _():
        store(ys_ref)


def _combine(x1, route, eo, dest3, g, tp, ts):
    t = tp + ts
    ntiles, _, tm = dest3.shape
    npt = tp // tm
    return pl.pallas_call(
        functools.partial(_combine_kernel, tm=tm, n_prompt_tiles=npt, n_tiles=ntiles),
        grid=(ntiles,),
        in_specs=[
            pl.BlockSpec((1, 2, tm), lambda i: (i, 0, 0), memory_space=pltpu.SMEM),
            pl.BlockSpec((1, 2, tm), lambda i: (jnp.minimum(i + 1, ntiles - 1), 0, 0), memory_space=pltpu.SMEM),
            pl.BlockSpec((tm, D_MODEL), lambda i: (i, 0)),
            pl.BlockSpec((tm, LANES), lambda i: (i, 0)),
            pl.BlockSpec((1, D_MODEL), lambda i: (0, 0)),
            pl.BlockSpec(memory_space=pl.ANY),
        ],
        out_specs=[
            pl.BlockSpec((tm, D_MODEL), lambda i: (jnp.minimum(i, npt - 1), 0)),
            pl.BlockSpec((tm, D_MODEL), lambda i: (jnp.maximum(i - npt, 0), 0)),
        ],
        out_shape=[jax.ShapeDtypeStruct((tp, D_MODEL), F32),
                   jax.ShapeDtypeStruct((ts, D_MODEL), F32)],
        scratch_shapes=[pltpu.VMEM((2, 2, tm, HALF_D), U32), pltpu.SemaphoreType.DMA((2,))],
        compiler_params=_cparams(("arbitrary",)),
        name="combine",
    )(dest3, dest3, x1, route, g, eo)


def _tile(n, pref):
    t = min(pref, n)
    assert n % t == 0, (n, t)
    return t


def _layer(xp, xs, seq_lens, norm_mix_g, norm_ffn_g, norm_final_g, w_in, b_cg, gmlp_ln_g, gmlp_w_s,
           gmlp_b_s, conv_w, head_g, w_out, w_rg, b_rg, w_re, b_re, w_eg, w_eu, w_ed):
    tp, ts = xp.shape[0], xs.shape[0]
    t = tp + ts
    row2 = lambda a: a.reshape(1, -1).astype(F32)

    w_main = jnp.concatenate([w_in[:, :OFF_CG], w_in[:, OFF_MERGE:]], axis=1).astype(BF16)
    w_cg = jnp.pad(w_in[:, OFF_CG:OFF_MERGE], ((0, 0), (0, LANES - N_CELL_GATES))).astype(BF16)
    bias_cg = jnp.pad(b_cg.astype(F32), (0, LANES - N_CELL_GATES)).reshape(1, LANES)
    bst = jnp.pad(gmlp_b_s.astype(F32).T, ((0, 0), (0, LANES - G_GROUPS)))
    conv_w8 = jnp.pad(conv_w.astype(F32), ((0, 3), (0, 0)))
    w_router = jnp.pad(jnp.concatenate([w_rg, w_re], axis=1).astype(F32),
                       ((0, 0), (0, LANES - N_GROUPS - N_EXPERTS)))
    b_router = jnp.pad(jnp.concatenate([b_rg, b_re]).astype(F32),
                       (0, LANES - N_GROUPS - N_EXPERTS)).reshape(1, LANES)

    tm = _tile(np.gcd(tp, ts), 1024)
    conv_rows = _tile(np.gcd.reduce(seq_lens), 512)
    starts = np.cumsum([0] + list(seq_lens))
    n_ct = t // conv_rows
    flags = np.zeros((n_ct, 2), np.int32)
    for s0, s1 in zip(starts[:-1], starts[1:]):
        flags[s0 // conv_rows, 0] = 1
        flags[s1 // conv_rows - 1, 1] = 1
    nchunks = t // CHUNK
    first = np.zeros((nchunks,), np.int32)
    last = np.zeros((nchunks,), np.int32)
    for s0, s1 in zip(starts[:-1], starts[1:]):
        first[s0 // CHUNK] = 1
        last[s1 // CHUNK - 1] = 1
    fwd_order = np.arange(nchunks, dtype=np.int32)
    bwd_order = fwd_order[::-1].copy()

    z, cg = _proj(xp, xs, row2(norm_mix_g), w_main, w_cg, tm, _tile(Z_COLS, 1024))
    ga = _gmlp(z, row2(gmlp_ln_g), gmlp_w_s.astype(BF16), bst, _tile(np.gcd(tp, ts), 512))
    qc, kt = _conv(z, conv_w8, jnp.asarray(flags), conv_rows, 1024)
    hf = _mlstm(qc, kt, z, cg, bias_cg, jnp.asarray(first), jnp.asarray(fwd_order), reverse=False)
    merged = _mlstm(qc, kt, z, cg, bias_cg, jnp.asarray(last[::-1].copy()), jnp.asarray(bwd_order), reverse=True,
                    extras=(hf, ga, head_g.astype(F32)))
    x1, hp, logits = _outproj(merged, xp, xs, w_out.astype(BF16), row2(norm_ffn_g), w_router, b_router,
                              _tile(np.gcd(tp, ts), 512))

    route, counts = _route(logits, _tile(t, 512))
    bm = EXPERT_BLOCK
    nblocks = (2 * t) // bm + N_EXPERTS
    counts = counts[0, :N_EXPERTS].astype(jnp.int32)
    padded = ((counts + bm - 1) // bm) * bm
    pad_end = jnp.cumsum(padded)
    pad_start = pad_end - padded
    e_id = route[:, R_E0:R_E1 + 1].astype(jnp.int32)
    rank = route[:, R_RANK0:R_RANK1 + 1].astype(jnp.int32)
    e_iota = jnp.arange(N_EXPERTS, dtype=jnp.int32)
    start_of = jnp.sum(jnp.where(e_id[:, :, None] == e_iota, pad_start, 0), axis=-1)
    dest = start_of + rank
    block_start = jnp.arange(nblocks, dtype=jnp.int32) * bm
    n_used = (pad_end[-1] // bm).astype(jnp.int32).reshape(1)
    last_start = pad_end[-1] - bm
    block_e = jnp.sum(jnp.minimum(block_start, last_start)[:, None] >= pad_end[None, :], axis=1).astype(jnp.int32)
    tiles = lambda tm: dest.reshape(t // tm, tm, 2).transpose(0, 2, 1)
    xs = _dispatch(hp, tiles(_tile(np.gcd(tp, ts), DISPATCH_ROWS)), pad_start + counts, padded - counts,
                   n_used, bm, nblocks)
    eo = _experts(xs, block_e, n_used, w_eg.astype(BF16), w_eu.astype(BF16), w_ed.astype(BF16), bm)
    return _combine(x1, route, eo, tiles(_tile(np.gcd(tp, ts), COMBINE_ROWS)), row2(norm_final_g), tp, ts)


def kernel(x_prompt, x_sample, norm_mix_g, norm_ffn_g, norm_final_g, w_in, b_cell_gates, gmlp_ln_g, gmlp_w_s,
           gmlp_b_s, mlstm_conv_w, mlstm_head_g, w_out, w_router_group, b_router_group, w_router_expert,
           b_router_expert, w_exp_gate, w_exp_up, w_exp_down):
    bp, sp, d = x_prompt.shape
    bs, ss, _ = x_sample.shape
    seq_lens = [sp] * bp + [ss] * bs
    yp, ys = _layer(
        x_prompt.reshape(bp * sp, d), x_sample.reshape(bs * ss, d), seq_lens,
        norm_mix_g[0], norm_ffn_g[0], norm_final_g, w_in[0], b_cell_gates[0], gmlp_ln_g[0], gmlp_w_s[0],
        gmlp_b_s[0], mlstm_conv_w[0], mlstm_head_g[0], w_out[0], w_router_group[0], b_router_group[0],
        w_router_expert[0], b_router_expert[0], w_exp_gate[0], w_exp_up[0], w_exp_down[0])
    return yp.reshape(bp, sp, d), ys.reshape(bs, ss, d)
```

```python
import functools

import jax
import jax.numpy as jnp
import numpy as np
from jax import lax
from jax.experimental import pallas as pl
from jax.experimental.pallas import tpu as pltpu

F32 = jnp.float32
BF16 = jnp.bfloat16
U32 = jnp.uint32
HI_HALF = np.uint32(0xFFFF0000)

D_MODEL = 2048
HALF_D = D_MODEL // 2
CHUNK = 128
G_GROUPS = 8
G_GDIM = D_MODEL // G_GROUPS
M_HEADS = 8
M_HDIM = D_MODEL // M_HEADS
CONV_W = 5
N_GROUPS = 4
EXPERTS_PER_GROUP = 8
N_EXPERTS = N_GROUPS * EXPERTS_PER_GROUP
D_EXPERT = 1024
EPS = 1e-6
LANES = 128
SUBLANES = 8
N_CELL_GATES = 4 * M_HEADS
OFF_CG = 6 * D_MODEL
OFF_MERGE = OFF_CG + N_CELL_GATES

ZG_GU, ZG_GV = range(2)
ZP_Q, ZP_K, ZP_V = range(3)
ZS_O, ZS_MA, ZS_MB = range(3)
PROJ_SUB = 256

V_AUG = M_HDIM + LANES
VMEM_LIMIT = 56 * 1024 * 1024
EXPERT_BLOCK = 512
DISPATCH_ROWS = 1024
COMBINE_ROWS = 256


def _cparams(sem):
    return pltpu.CompilerParams(dimension_semantics=sem, vmem_limit_bytes=VMEM_LIMIT)


def _rms(x, g):
    ms = jnp.mean(x * x, axis=-1, keepdims=True)
    return x * lax.rsqrt(ms + EPS) * g


def _sigmoid(x):
    return 1.0 / (1.0 + jnp.exp(-x))


def _gelu_tanh(x):
    c = np.float32(np.sqrt(2.0 / np.pi))
    return 0.5 * x * (1.0 + jnp.tanh(c * (x + np.float32(0.044715) * (x * x * x))))


def _matmul_act(h_ref, w_ref, z_ref, act):
    for c in range(w_ref.shape[1] // PROJ_SUB):
        cs = slice(c * PROJ_SUB, (c + 1) * PROJ_SUB)
        acc = jnp.dot(h_ref[...], w_ref[:, cs], preferred_element_type=F32)
        z_ref[:, cs] = act(acc).astype(BF16)


def _proj_first_kernel(xp_ref, xs_ref, g_ref, w_ref, wcg_ref, z_ref, cg_ref, h_ref, *, n_prompt_tiles):
    i = pl.program_id(0)

    @pl.when(pl.program_id(1) == 0)
    def _():
        @pl.when(i < n_prompt_tiles)
        def _():
            h_ref[...] = _rms(xp_ref[...], g_ref[...]).astype(BF16)

        @pl.when(i >= n_prompt_tiles)
        def _():
            h_ref[...] = _rms(xs_ref[...], g_ref[...]).astype(BF16)

        cg_ref[...] = jnp.dot(h_ref[...], wcg_ref[...], preferred_element_type=F32)

    _matmul_act(h_ref, w_ref, z_ref, lambda a: a)


def _proj_first(xp, xs, g, w, wcg, tm, tn):
    tp, ts = xp.shape[0], xs.shape[0]
    t = tp + ts
    npt, nst = tp // tm, ts // tm
    return pl.pallas_call(
        functools.partial(_proj_first_kernel, n_prompt_tiles=npt),
        grid=(npt + nst, w.shape[1] // tn),
        in_specs=[
            pl.BlockSpec((tm, D_MODEL), lambda i, j: (jnp.minimum(i, npt - 1), 0)),
            pl.BlockSpec((tm, D_MODEL), lambda i, j: (jnp.maximum(i - npt, 0), 0)),
            pl.BlockSpec((1, D_MODEL), lambda i, j: (0, 0)),
            pl.BlockSpec((D_MODEL, tn), lambda i, j: (0, j)),
            pl.BlockSpec((D_MODEL, LANES), lambda i, j: (0, 0)),
        ],
        out_specs=[
            pl.BlockSpec((tm, tn), lambda i, j: (i, j)),
            pl.BlockSpec((tm, LANES), lambda i, j: (i, 0)),
            pl.BlockSpec((tm, D_MODEL), lambda i, j: (i, 0)),
        ],
        out_shape=[
            jax.ShapeDtypeStruct((t, w.shape[1]), BF16),
            jax.ShapeDtypeStruct((t, LANES), F32),
            jax.ShapeDtypeStruct((t, D_MODEL), BF16),
        ],
        compiler_params=_cparams(("arbitrary", "arbitrary")),
        name="proj_plain",
    )(xp, xs, g, w, wcg)


def _proj_act_kernel(h_ref, w_ref, z_ref, *, act):
    _matmul_act(h_ref, w_ref, z_ref, act)


def _proj_act(h, w, act, name, tm, tn):
    t = h.shape[0]
    return pl.pallas_call(
        functools.partial(_proj_act_kernel, act=act),
        grid=(t // tm, w.shape[1] // tn),
        in_specs=[
            pl.BlockSpec((tm, D_MODEL), lambda i, j: (i, 0)),
            pl.BlockSpec((D_MODEL, tn), lambda i, j: (0, j)),
        ],
        out_specs=pl.BlockSpec((tm, tn), lambda i, j: (i, j)),
        out_shape=jax.ShapeDtypeStruct((t, w.shape[1]), BF16),
        compiler_params=_cparams(("arbitrary", "arbitrary")),
        name=name,
    )(h, w)


def _gmlp_kernel(gu_ref, gv_ref, ma_ref, lng_ref, ws_ref, bst_ref, out_ref, vn_ref, *, rows):
    v = gv_ref[...].astype(F32)
    mu = jnp.mean(v, axis=-1, keepdims=True)
    vc = v - mu
    var = jnp.mean(vc * vc, axis=-1, keepdims=True)
    vn_ref[...] = (vc * lax.rsqrt(var + EPS) * lng_ref[...]).astype(BF16)
    for c in range(rows // CHUNK):
        rs = slice(c * CHUNK, (c + 1) * CHUNK)
        for g in range(G_GROUPS):
            cs = slice(g * G_GDIM, (g + 1) * G_GDIM)
            s = jnp.dot(ws_ref[g], vn_ref[rs, cs], preferred_element_type=F32) + bst_ref[:, g:g + 1]
            a = gu_ref[rs, cs].astype(F32) * s
            out_ref[rs, cs] = (ma_ref[rs, cs].astype(F32) * a).astype(BF16)


def _gmlp(zg, zs, lng, ws, bst, rows):
    t = zg.shape[0]
    zspec = lambda b: pl.BlockSpec((rows, D_MODEL), lambda i, b=b: (i, b))
    return pl.pallas_call(
        functools.partial(_gmlp_kernel, rows=rows),
        grid=(t // rows,),
        in_specs=[
            zspec(ZG_GU), zspec(ZG_GV), zspec(ZS_MA),
            pl.BlockSpec((1, D_MODEL), lambda i: (0, 0)),
            pl.BlockSpec((G_GROUPS, CHUNK, CHUNK), lambda i: (0, 0, 0)),
            pl.BlockSpec((CHUNK, LANES), lambda i: (0, 0)),
        ],
        out_specs=pl.BlockSpec((rows, D_MODEL), lambda i: (i, 0)),
        out_shape=jax.ShapeDtypeStruct((t, D_MODEL), BF16),
        scratch_shapes=[pltpu.VMEM((rows, D_MODEL), BF16)],
        compiler_params=_cparams(("arbitrary",)),
        name="gmlp",
    )(zg, zg, zs, lng, ws, bst)


HALO = 16


def _conv_kernel(flags_ref, prev_ref, cur_ref, next_ref, w_ref, q_ref, kt_ref, *, rows, n_q_blocks):
    i = pl.program_id(0)
    j = pl.program_id(1)
    x = cur_ref[...].astype(F32)
    keep_prev = (flags_ref[i, 0] == 0).astype(F32)
    keep_next = (flags_ref[i, 1] == 0).astype(F32)
    p = prev_ref[...].astype(F32) * keep_prev
    n = next_ref[...].astype(F32) * keep_next
    row = lax.broadcasted_iota(jnp.int32, x.shape, 0)
    w = w_ref[...]
    half = CONV_W // 2
    acc = x * w[half:half + 1, :]
    for d in range(1, half + 1):
        xm = pltpu.roll(x, d, 0)
        for r in range(d):
            xm = jnp.where(row == r, p[HALO - d + r:HALO - d + r + 1, :], xm)
        acc = acc + xm * w[half - d:half - d + 1, :]
        xp = pltpu.roll(x, rows - d, 0)
        for r in range(d):
            xp = jnp.where(row == rows - d + r, n[r:r + 1, :], xp)
        acc = acc + xp * w[half + d:half + d + 1, :]
    y = acc * _sigmoid(acc)

    @pl.when(j < n_q_blocks)
    def _():
        q_ref[...] = y.astype(BF16)

    @pl.when(j >= n_q_blocks)
    def _():
        kt_ref[...] = (y * np.float32(M_HDIM ** -0.5)).T.astype(BF16)


def _conv(z, w, flags, rows, cols):
    t = z.shape[0]
    nqb = D_MODEL // cols
    cb0 = ZP_Q * D_MODEL // cols
    hb = rows // HALO
    nhb = t // HALO
    kern = functools.partial(_conv_kernel, rows=rows, n_q_blocks=nqb)
    grid_spec = pltpu.PrefetchScalarGridSpec(
        num_scalar_prefetch=1,
        grid=(t // rows, 2 * nqb),
        in_specs=[
            pl.BlockSpec((HALO, cols), lambda i, j, f: (jnp.maximum(i * hb - 1, 0), cb0 + j)),
            pl.BlockSpec((rows, cols), lambda i, j, f: (i, cb0 + j)),
            pl.BlockSpec((HALO, cols), lambda i, j, f: (jnp.minimum((i + 1) * hb, nhb - 1), cb0 + j)),
            pl.BlockSpec((CONV_W + 3, cols), lambda i, j, f: (0, j)),
        ],
        out_specs=[
            pl.BlockSpec((rows, cols), lambda i, j, f: (i, jnp.minimum(j, nqb - 1))),
            pl.BlockSpec((cols, rows), lambda i, j, f: (jnp.maximum(j - nqb, 0), i)),
        ],
    )
    return pl.pallas_call(
        kern,
        grid_spec=grid_spec,
        out_shape=[jax.ShapeDtypeStruct((t, D_MODEL), BF16), jax.ShapeDtypeStruct((D_MODEL, t), BF16)],
        compiler_params=_cparams(("arbitrary", "arbitrary")),
        name="conv",
    )(flags, z, z, z, w)


def _log_sigmoid(x):
    return jnp.minimum(x, 0.0) - jnp.log1p(jnp.exp(-jnp.abs(x)))


def _running_max_lanes(x, reverse):
    n = x.shape[1]
    lane = lax.broadcasted_iota(jnp.int32, x.shape, 1)
    shift = 1
    while shift < n:
        if reverse:
            moved = jnp.where(lane < n - shift, pltpu.roll(x, n - shift, 1), -jnp.inf)
        else:
            moved = jnp.where(lane >= shift, pltpu.roll(x, shift, 1), -jnp.inf)
        x = jnp.maximum(x, moved)
        shift *= 2
    return x


def _mlstm_kernel(reset_ref, order_ref, *refs, reverse):
    del order_ref
    if reverse:
        (q_ref, kt_ref, v_ref, cg_ref, bias_ref, ones_ref, hf_ref, o_ref, mb_ref, ga_ref, hg_ref,
         out_ref, c_ref, m_ref) = refs
    else:
        q_ref, kt_ref, v_ref, cg_ref, bias_ref, ones_ref, out_ref, c_ref, m_ref = refs
    step = pl.program_id(0)

    @pl.when(reset_ref[step] == 1)
    def _():
        c_ref[...] = jnp.zeros_like(c_ref)
        m_ref[...] = jnp.zeros_like(m_ref)

    L = CHUNK
    gates_t = (cg_ref[...] + bias_ref[...]).T
    lf_t = _log_sigmoid(gates_t)
    r_i = lax.broadcasted_iota(jnp.int32, (L, L), 0)
    c_i = lax.broadcasted_iota(jnp.int32, (L, L), 1)
    if reverse:
        valid = c_i >= r_i
        i_off, f_off, last = 2 * M_HEADS, 3 * M_HEADS, 0
    else:
        valid = c_i <= r_i
        i_off, f_off, last = 0, M_HEADS, L - 1
    tri = valid.astype(F32)
    f_rows = lax.dot_general(lf_t, tri, (((1,), (1,)), ((), ())), precision=lax.Precision.HIGHEST,
                             preferred_element_type=F32)[f_off:f_off + M_HEADS, :]
    b_rows = gates_t[i_off:i_off + M_HEADS, :] - f_rows
    cm_rows = _running_max_lanes(b_rows, reverse)
    cols = jnp.concatenate([cm_rows, f_rows, jnp.zeros((L - 2 * M_HEADS, L), F32)], axis=0).T
    heads = range(M_HEADS)
    hsl = [slice(h * M_HDIM, (h + 1) * M_HDIM) for h in heads]
    dot = functools.partial(jnp.dot, preferred_element_type=F32)
    q = [q_ref[:, hsl[h]] for h in heads]
    kt = [kt_ref[hsl[h], :] for h in heads]
    c_old = [c_ref[h] for h in heads]
    s = [dot(q[h], kt[h]) for h in heads]
    qc = [dot(q[h], c_old[h].astype(BF16)) for h in heads]
    ones_blk = ones_ref[...]
    v_aug =[jnp.concatenate([v_ref[:, hsl[h]], ones_blk], axis=1) for h in heads]
    brow = [b_rows[h:h + 1, :] for h in heads]
    f_last = [f_rows[h:h + 1, last:last + 1] for h in heads]
    m_old = [m_ref[h][:, 0:1] for h in heads]
    u = [jnp.maximum(m_old[h], cols[:, h:h + 1]) for h in heads]

    p =[(s[h] * jnp.exp(jnp.where(valid, brow[h] - u[h], -jnp.inf))).astype(BF16) for h in heads]
    nd = [jnp.exp(m_old[h] - u[h]) * qc[h] + dot(p[h], v_aug[h]) for h in heads]

    for h in heads:
        g_row = f_last[h] + brow[h]
        m_new = jnp.maximum(f_last[h] + m_old[h], jnp.max(g_row, axis=-1, keepdims=True))
        a = jnp.exp(f_last[h] + m_old[h] - m_new)
        kwt = (kt[h].astype(F32) * jnp.exp(g_row - m_new)).astype(BF16)
        c_ref[h] = a * c_old[h] + dot(kwt, v_aug[h])
        m_ref[h] = jnp.broadcast_to(m_new, (1, LANES))

    for h in heads:
        f_col = cols[:, M_HEADS + h:M_HEADS + h + 1]
        den = nd[h][:, M_HDIM:M_HDIM + 1]
        hout = nd[h][:, :M_HDIM] * (1.0 / jnp.maximum(jnp.abs(den), jnp.exp(-(f_col + u[h]))))
        if reverse:
            hsum = hf_ref[:, hsl[h]].astype(F32) + hout
            hn = _rms(hsum, hg_ref[h:h + 1, :])
            b = hn * o_ref[:, hsl[h]].astype(F32)
            merged = ga_ref[:, hsl[h]].astype(F32) + mb_ref[:, hsl[h]].astype(F32) * b
            out_ref[:, hsl[h]] = merged.astype(BF16)
        else:
            out_ref[:, hsl[h]] = hout.astype(BF16)


def _mlstm(qc, kt, zp, zs, cg, bias, reset, order, reverse, extras=None):
    t = zp.shape[0]
    nchunks = t // CHUNK
    row = lambda b: pl.BlockSpec((CHUNK, D_MODEL), lambda i, rs, od, b=b: (od[i], b))
    in_specs = [row(0),
                pl.BlockSpec((D_MODEL, CHUNK), lambda i, rs, od: (0, od[i])),
                row(ZP_V),
                pl.BlockSpec((CHUNK, LANES), lambda i, rs, od: (od[i], 0)),
                pl.BlockSpec((1, LANES), lambda i, rs, od: (0, 0)),
                pl.BlockSpec((CHUNK, LANES), lambda i, rs, od: (0, 0))]
    ones_col = jnp.zeros((CHUNK, LANES), BF16).at[:, 0].set(1)
    args = [qc, kt, zp, cg, bias, ones_col]
    if reverse:
        hf, ga, hg = extras
        in_specs += [row(0), row(ZS_O), row(ZS_MB), row(0),
                     pl.BlockSpec((M_HEADS, M_HDIM), lambda i, rs, od: (0, 0))]
        args += [hf, zs, zs, ga, hg]
    grid_spec = pltpu.PrefetchScalarGridSpec(
        num_scalar_prefetch=2,
        grid=(nchunks,),
        in_specs=in_specs,
        out_specs=row(0),
        scratch_shapes=[pltpu.VMEM((M_HEADS, M_HDIM, V_AUG), F32),
                        pltpu.VMEM((M_HEADS, 1, LANES), F32)],
    )
    return pl.pallas_call(
        functools.partial(_mlstm_kernel, reverse=reverse),
        grid_spec=grid_spec,
        out_shape=jax.ShapeDtypeStruct((t, D_MODEL), BF16),
        compiler_params=_cparams(("arbitrary",)),
        name="mlstm_bwd" if reverse else "mlstm_fwd",
    )(reset, order, *args)


def _pack_bf16_pair(lo, hi):
    lo_bits = pltpu.bitcast(lo.astype(F32), U32) >> 16
    hi_bits = pltpu.bitcast(hi.astype(F32), U32) & HI_HALF
    return hi_bits | lo_bits


def _unpack_bf16_pair(p):
    lo = pltpu.bitcast(p << 16, F32)
    hi = pltpu.bitcast(p & HI_HALF, F32)
    return lo.astype(BF16), hi.astype(BF16)


def _outproj_kernel(m_ref, xp_ref, xs_ref, w_ref, g_ref, wr_hi_ref, wr_lo_ref, br_ref, x1_ref, hp_ref, lg_ref,
                    *, n_prompt_tiles):
    i = pl.program_id(0)
    y = jnp.dot(m_ref[...], w_ref[...], preferred_element_type=F32)

    def finish(x):
        x1 = x + y
        x1_ref[...] = x1
        hn = _rms(x1, g_ref[...])
        h_hi = hn.astype(BF16)
        h_lo = (hn - h_hi.astype(F32)).astype(BF16)
        hp_ref[...] = _pack_bf16_pair(h_hi[:, :HALF_D], h_hi[:, HALF_D:])
        dot = functools.partial(jnp.dot, preferred_element_type=F32)
        lg_ref[...] = (dot(h_hi, wr_hi_ref[...]) + dot(h_hi, wr_lo_ref[...]) + dot(h_lo, wr_hi_ref[...])
                       + br_ref[...])

    @pl.when(i < n_prompt_tiles)
    def _():
        finish(xp_ref[...])

    @pl.when(i >= n_prompt_tiles)
    def _():
        finish(xs_ref[...])


def _outproj(merged, xp, xs, w, g, wr, br, tm):
    t = merged.shape[0]
    npt = xp.shape[0] // tm
    full = lambda shape: pl.BlockSpec(shape, lambda i: (0, 0))
    wr_hi = wr.astype(BF16)
    wr_lo = (wr - wr_hi.astype(F32)).astype(BF16)
    return pl.pallas_call(
        functools.partial(_outproj_kernel, n_prompt_tiles=npt),
        grid=(t // tm,),
        in_specs=[
            pl.BlockSpec((tm, D_MODEL), lambda i: (i, 0)),
            pl.BlockSpec((tm, D_MODEL), lambda i: (jnp.minimum(i, npt - 1), 0)),
            pl.BlockSpec((tm, D_MODEL), lambda i: (jnp.maximum(i - npt, 0), 0)),
            full((D_MODEL, D_MODEL)), full((1, D_MODEL)), full((D_MODEL, LANES)), full((D_MODEL, LANES)),
            full((1, LANES)),
        ],
        out_specs=[
            pl.BlockSpec((tm, D_MODEL), lambda i: (i, 0)),
            pl.BlockSpec((tm, HALF_D), lambda i: (i, 0)),
            pl.BlockSpec((tm, LANES), lambda i: (i, 0)),
        ],
        out_shape=[
            jax.ShapeDtypeStruct((t, D_MODEL), F32),
            jax.ShapeDtypeStruct((t, HALF_D), U32),
            jax.ShapeDtypeStruct((t, LANES), F32),
        ],
        compiler_params=_cparams(("arbitrary",)),
        name="outproj",
    )(merged, xp, xs, w, g, wr_hi, wr_lo, br)


R_W0, R_W1, R_E0, R_E1, R_RANK0, R_RANK1 = range(6)


def _route_kernel(lg_ref, rt_ref, cnt_ref, carry_ref, *, rows):
    i = pl.program_id(0)

    @pl.when(i == 0)
    def _():
        carry_ref[...] = jnp.zeros_like(carry_ref)

    lg = lg_ref[...]
    lane = lax.broadcasted_iota(jnp.int32, lg.shape, 1).astype(F32)
    big = np.float32(LANES)
    neg = -jnp.inf
    rmax = lambda a: jnp.max(a, axis=-1, keepdims=True)
    rmin = lambda a: jnp.min(a, axis=-1, keepdims=True)

    gmask = lane < N_GROUPS
    gl = jnp.where(gmask, lg, neg)
    gmax = rmax(gl)
    gsel = rmin(jnp.where(gl == gmax, lane, big))
    p_g = 1.0 / jnp.sum(jnp.where(gmask, jnp.exp(gl - gmax), 0.0), axis=-1, keepdims=True)

    lo = N_GROUPS + EXPERTS_PER_GROUP * gsel
    emask = jnp.logical_and(lane >= lo, lane < lo + EXPERTS_PER_GROUP)
    el = jnp.where(emask, lg, neg)
    v0 = rmax(el)
    i0 = rmin(jnp.where(el == v0, lane, big))
    el1 = jnp.where(lane == i0, neg, el)
    v1 = rmax(el1)
    i1 = rmin(jnp.where(el1 == v1, lane, big))
    ex = jnp.exp(v1 - v0)
    w0 = p_g / (1.0 + ex)
    w1 = p_g * ex / (1.0 + ex)
    e0 = i0 - N_GROUPS
    e1 = i1 - N_GROUPS

    oh0 = lane == e0
    oh1 = lane == e1
    onehot = jnp.logical_or(oh0, oh1)
    r_i = lax.broadcasted_iota(jnp.int32, (rows, rows), 0)
    c_i = lax.broadcasted_iota(jnp.int32, (rows, rows), 1)
    strict = (c_i < r_i).astype(BF16)
    before = jnp.dot(strict, onehot.astype(BF16), preferred_element_type=F32) + carry_ref[0:1, :]
    rank0 = jnp.sum(jnp.where(oh0, before, 0.0), axis=-1, keepdims=True)
    rank1 = jnp.sum(jnp.where(oh1, before, 0.0), axis=-1, keepdims=True)
    total = carry_ref[0:1, :] + jnp.sum(onehot.astype(F32), axis=0, keepdims=True)
    carry_ref[...] = jnp.broadcast_to(total, carry_ref.shape)
    cnt_ref[...] = jnp.broadcast_to(total, cnt_ref.shape)

    rec = jnp.zeros(lg.shape, F32)
    for idx, val in ((R_W0, w0), (R_W1, w1), (R_E0, e0), (R_E1, e1), (R_RANK0, rank0), (R_RANK1, rank1)):
        rec = jnp.where(lane == idx, val, rec)
    rt_ref[...] = rec


def _route(logits, rows):
    t = logits.shape[0]
    return pl.pallas_call(
        functools.partial(_route_kernel, rows=rows),
        grid=(t // rows,),
        in_specs=[pl.BlockSpec((rows, LANES), lambda i: (i, 0))],
        out_specs=[pl.BlockSpec((rows, LANES), lambda i: (i, 0)),
                   pl.BlockSpec((8, LANES), lambda i: (0, 0))],
        out_shape=[jax.ShapeDtypeStruct((t, LANES), F32),
                   jax.ShapeDtypeStruct((8, LANES), F32)],
        scratch_shapes=[pltpu.VMEM((8, LANES), F32)],
        compiler_params=_cparams(("arbitrary",)),
        name="route",
    )(logits)


def _dispatch_kernel(zstart_ref, zcount_ref, nused_ref, dest_ref, hp_ref, xs_hbm, zbuf, sem, zsem,
                     *, tm, bm, nblocks):
    i = pl.program_id(0)

    @pl.when(i == 0)
    def _():
        zbuf[...] = jnp.zeros_like(zbuf)

        def per_expert(e, _):
            start, n = zstart_ref[e], zcount_ref[e]

            def issue(r, _):
                pltpu.make_async_copy(zbuf.at[pl.ds(0, 1), :], xs_hbm.at[pl.ds(start + r, 1), :], zsem).start()
                return 0

            def wait(r, _):
                pltpu.make_async_copy(zbuf.at[pl.ds(0, 1), :], xs_hbm.at[pl.ds(start + r, 1), :], zsem).wait()
                return 0
            lax.fori_loop(0, n, issue, 0)
            lax.fori_loop(0, n, wait, 0)
            return 0
        lax.fori_loop(0, N_EXPERTS, per_expert, 0)

        def tail_copy(b):
            row0 = pl.multiple_of(b * bm, bm)
            return pltpu.make_async_copy(zbuf, xs_hbm.at[pl.ds(row0, bm), :], zsem)

        def tail_issue(b, _):
            tail_copy(b).start()
            return 0

        def tail_wait(b, _):
            tail_copy(b).wait()
            return 0
        lax.fori_loop(nused_ref[0], nblocks, tail_issue, 0)
        lax.fori_loop(nused_ref[0], nblocks, tail_wait, 0)

    def body(r, _):
        for k in range(2):
            pltpu.make_async_copy(hp_ref.at[pl.ds(r, 1), :], xs_hbm.at[pl.ds(dest_ref[0, k, r], 1), :],
                                  sem).start()
        return 0
    lax.fori_loop(0, tm, body, 0, unroll=8)
    for k in range(2):
        pltpu.make_async_copy(hp_ref, xs_hbm.at[pl.ds(0, tm), :], sem).wait()


def _dispatch(hp, dest3, zstart, zcount, n_used, bm, nblocks):
    ntiles, _, tm = dest3.shape
    grid_spec = pltpu.PrefetchScalarGridSpec(
        num_scalar_prefetch=3,
        grid=(ntiles,),
        in_specs=[
            pl.BlockSpec((1, 2, tm), lambda i, zs, zc, nu: (i, 0, 0), memory_space=pltpu.SMEM),
            pl.BlockSpec((tm, HALF_D), lambda i, zs, zc, nu: (i, 0)),
        ],
        out_specs=pl.BlockSpec(memory_space=pl.ANY),
        scratch_shapes=[
            pltpu.VMEM((bm, HALF_D), U32),
            pltpu.SemaphoreType.DMA(()),
            pltpu.SemaphoreType.DMA(()),
        ],
    )
    return pl.pallas_call(
        functools.partial(_dispatch_kernel, tm=tm, bm=bm, nblocks=nblocks),
        grid_spec=grid_spec,
        out_shape=jax.ShapeDtypeStruct((nblocks * bm, HALF_D), U32),
        compiler_params=_cparams(("arbitrary",)),
        name="dispatch",
    )(zstart, zcount, n_used, dest3, hp)


def _experts_kernel(be_ref, nused_ref, xs_ref, wg_ref, wu_ref, wd_ref, out_ref):
    del be_ref
    b = pl.program_id(0)

    @pl.when(b < nused_ref[0])
    def _():
        x_lo, x_hi = _unpack_bf16_pair(xs_ref[...])
        xb = jnp.concatenate([x_lo, x_hi], axis=1)
        hg = jnp.dot(xb, wg_ref[0], preferred_element_type=F32)
        hu = jnp.dot(xb, wu_ref[0], preferred_element_type=F32)
        hmid = (hg * _sigmoid(hg) * hu).astype(BF16)
        o = jnp.dot(hmid, wd_ref[0], preferred_element_type=F32).astype(BF16)
        out_ref[...] = _pack_bf16_pair(o[:, :HALF_D], o[:, HALF_D:])

    @pl.when(b >= nused_ref[0])
    def _():
        out_ref[...] = jnp.zeros_like(out_ref)


def _experts(xs, block_e, n_used, wg, wu, wd, bm):
    nblocks = block_e.shape[0]
    wspec = lambda shape: pl.BlockSpec((1,) + shape, lambda b, be, nu: (be[b], 0, 0))
    grid_spec = pltpu.PrefetchScalarGridSpec(
        num_scalar_prefetch=2,
        grid=(nblocks,),
        in_specs=[
            pl.BlockSpec((bm, HALF_D), lambda b, be, nu: (jnp.minimum(b, nu[0] - 1), 0)),
            wspec((D_MODEL, D_EXPERT)), wspec((D_MODEL, D_EXPERT)), wspec((D_EXPERT, D_MODEL)),
        ],
        out_specs=pl.BlockSpec((bm, HALF_D), lambda b, be, nu: (b, 0)),
    )
    return pl.pallas_call(
        _experts_kernel,
        grid_spec=grid_spec,
        out_shape=jax.ShapeDtypeStruct((nblocks * bm, HALF_D), U32),
        compiler_params=_cparams(("arbitrary",)),
        name="experts",
    )(block_e, n_used, xs, wg, wu, wd)


def _combine_kernel(dcur_ref, dnext_ref, x1_ref, rt_ref, g_ref, eo_hbm, yp_ref, ys_ref, gbuf, gsem,
                    *, tm, n_prompt_tiles, n_tiles):
    i = pl.program_id(0)
    slot = lax.rem(i, 2)

    def start_gather(idx_ref, sl):
        def body(r, _):
            for k in range(2):
                pltpu.make_async_copy(eo_hbm.at[pl.ds(idx_ref[0, k, r], 1), :], gbuf.at[sl, k, pl.ds(r, 1), :],
                                      gsem.at[sl]).start()
            return 0
        lax.fori_loop(0, tm, body, 0, unroll=8)

    @pl.when(i == 0)
    def _():
        start_gather(dcur_ref, 0)

    @pl.when(i + 1 < n_tiles)
    def _():
        start_gather(dnext_ref, 1 - slot)

    pltpu.make_async_copy(gbuf.at[slot], gbuf.at[slot], gsem.at[slot]).wait()
    rt = rt_ref[...]
    w0 = rt[:, R_W0:R_W0 + 1]
    w1 = rt[:, R_W1:R_W1 + 1]
    lo0, hi0 = _unpack_bf16_pair(gbuf[slot, 0])
    lo1, hi1 = _unpack_bf16_pair(gbuf[slot, 1])
    y_lo = x1_ref[:, :HALF_D] + w0 * lo0.astype(F32) + w1 * lo1.astype(F32)
    y_hi = x1_ref[:, HALF_D:] + w0 * hi0.astype(F32) + w1 * hi1.astype(F32)
    ms = (jnp.sum(y_lo * y_lo, axis=-1, keepdims=True) + jnp.sum(y_hi * y_hi, axis=-1, keepdims=True)) / D_MODEL
    scale = lax.rsqrt(ms + EPS)

    def store(y_ref):
        y_ref[:, :HALF_D] = y_lo * scale * g_ref[:, :HALF_D]
        y_ref[:, HALF_D:] = y_hi * scale * g_ref[:, HALF_D:]

    @pl.when(i < n_prompt_tiles)
    def _():
        store(yp_ref)

    @pl.when(i >= n_prompt_tiles)
    def _():
        store(ys_ref)


def _combine(x1, route, eo, dest3, g, tp, ts):
    t = tp + ts
    ntiles, _, tm = dest3.shape
    npt = tp // tm
    return pl.pallas_call(
        functools.partial(_combine_kernel, tm=tm, n_prompt_tiles=npt, n_tiles=ntiles),
        grid=(ntiles,),
        in_specs=[
            pl.BlockSpec((1, 2, tm), lambda i: (i, 0, 0), memory_space=pltpu.SMEM),
            pl.BlockSpec((1, 2, tm), lambda i: (jnp.minimum(i + 1, ntiles - 1), 0, 0), memory_space=pltpu.SMEM),
            pl.BlockSpec((tm, D_MODEL), lambda i: (i, 0)),
            pl.BlockSpec((tm, LANES), lambda i: (i, 0)),
            pl.BlockSpec((1, D_MODEL), lambda i: (0, 0)),
            pl.BlockSpec(memory_space=pl.ANY),
        ],
        out_specs=[
            pl.BlockSpec((tm, D_MODEL), lambda i: (jnp.minimum(i, npt - 1), 0)),
            pl.BlockSpec((tm, D_MODEL), lambda i: (jnp.maximum(i - npt, 0), 0)),
        ],
        out_shape=[jax.ShapeDtypeStruct((tp, D_MODEL), F32),
                   jax.ShapeDtypeStruct((ts, D_MODEL), F32)],
        scratch_shapes=[pltpu.VMEM((2, 2, tm, HALF_D), U32), pltpu.SemaphoreType.DMA((2,))],
        compiler_params=_cparams(("arbitrary",)),
        name="combine",
    )(dest3, dest3, x1, route, g, eo)


def _tile(n, pref):
    t = min(pref, n)
    assert n % t == 0, (n, t)
    return t


def _layer(xp, xs, seq_lens, norm_mix_g, norm_ffn_g, norm_final_g, w_in, b_cg, gmlp_ln_g, gmlp_w_s,
           gmlp_b_s, conv_w, head_g, w_out, w_rg, b_rg, w_re, b_re, w_eg, w_eu, w_ed):
    tp, ts = xp.shape[0], xs.shape[0]
    t = tp + ts
    row2 = lambda a: a.reshape(1, -1).astype(F32)

    w_gelu = w_in[:, :2 * D_MODEL].astype(BF16)
    w_plain = w_in[:, 2 * D_MODEL:5 * D_MODEL].astype(BF16)
    w_sig = jnp.concatenate([w_in[:, 5 * D_MODEL:OFF_CG], w_in[:, OFF_MERGE:]], axis=1).astype(BF16)
    w_cg = jnp.pad(w_in[:, OFF_CG:OFF_MERGE], ((0, 0), (0, LANES - N_CELL_GATES))).astype(BF16)
    bias_cg = jnp.pad(b_cg.astype(F32), (0, LANES - N_CELL_GATES)).reshape(1, LANES)
    bst = jnp.pad(gmlp_b_s.astype(F32).T, ((0, 0), (0, LANES - G_GROUPS)))
    conv_w8 = jnp.pad(conv_w.astype(F32), ((0, 3), (0, 0)))
    w_router = jnp.pad(jnp.concatenate([w_rg, w_re], axis=1).astype(F32),
                       ((0, 0), (0, LANES - N_GROUPS - N_EXPERTS)))
    b_router = jnp.pad(jnp.concatenate([b_rg, b_re]).astype(F32),
                       (0, LANES - N_GROUPS - N_EXPERTS)).reshape(1, LANES)

    tm = _tile(np.gcd(tp, ts), 1024)
    conv_rows = _tile(np.gcd.reduce(seq_lens), 512)
    starts = np.cumsum([0] + list(seq_lens))
    n_ct = t // conv_rows
    flags = np.zeros((n_ct, 2), np.int32)
    for s0, s1 in zip(starts[:-1], starts[1:]):
        flags[s0 // conv_rows, 0] = 1
        flags[s1 // conv_rows - 1, 1] = 1
    nchunks = t // CHUNK
    first = np.zeros((nchunks,), np.int32)
    last = np.zeros((nchunks,), np.int32)
    for s0, s1 in zip(starts[:-1], starts[1:]):
        first[s0 // CHUNK] = 1
        last[s1 // CHUNK - 1] = 1
    fwd_order = np.arange(nchunks, dtype=np.int32)
    bwd_order = fwd_order[::-1].copy()

    tn = 1024
    zp, cg, h = _proj_first(xp, xs, row2(norm_mix_g), w_plain, w_cg, tm, tn)
    zg = _proj_act(h, w_gelu, _gelu_tanh, "proj_gelu", tm, tn)
    zs = _proj_act(h, w_sig, _sigmoid, "proj_sigmoid", tm, tn)
    ga = _gmlp(zg, zs, row2(gmlp_ln_g), gmlp_w_s.astype(BF16), bst, _tile(np.gcd(tp, ts), 512))
    qc, kt = _conv(zp, conv_w8, jnp.asarray(flags), conv_rows, 1024)
    hf = _mlstm(qc, kt, zp, zs, cg, bias_cg, jnp.asarray(first), jnp.asarray(fwd_order), reverse=False)
    merged = _mlstm(qc, kt, zp, zs, cg, bias_cg, jnp.asarray(last[::-1].copy()), jnp.asarray(bwd_order),
                    reverse=True, extras=(hf, ga, head_g.astype(F32)))
    x1, hp, logits = _outproj(merged, xp, xs, w_out.astype(BF16), row2(norm_ffn_g), w_router, b_router,
                              _tile(np.gcd(tp, ts), 512))

    route, counts = _route(logits, _tile(t, 512))
    bm = EXPERT_BLOCK
    nblocks = (2 * t) // bm + N_EXPERTS
    counts = counts[0, :N_EXPERTS].astype(jnp.int32)
    padded = ((counts + bm - 1) // bm) * bm
    pad_end = jnp.cumsum(padded)
    pad_start = pad_end - padded
    e_id = route[:, R_E0:R_E1 + 1].astype(jnp.int32)
    rank = route[:, R_RANK0:R_RANK1 + 1].astype(jnp.int32)
    e_iota = jnp.arange(N_EXPERTS, dtype=jnp.int32)
    start_of = jnp.sum(jnp.where(e_id[:, :, None] == e_iota, pad_start, 0), axis=-1)
    dest = start_of + rank
    block_start = jnp.arange(nblocks, dtype=jnp.int32) * bm
    n_used = (pad_end[-1] // bm).astype(jnp.int32).reshape(1)
    last_start = pad_end[-1] - bm
    block_e = jnp.sum(jnp.minimum(block_start, last_start)[:, None] >= pad_end[None, :], axis=1).astype(jnp.int32)
    tiles = lambda tm: dest.reshape(t // tm, tm, 2).transpose(0, 2, 1)
    xs = _dispatch(hp, tiles(_tile(np.gcd(tp, ts), DISPATCH_ROWS)), pad_start + counts, padded - counts,
                   n_used, bm, nblocks)
    eo = _experts(xs, block_e, n_used, w_eg.astype(BF16), w_eu.astype(BF16), w_ed.astype(BF16), bm)
    return _combine(x1, route, eo, tiles(_tile(np.gcd(tp, ts), COMBINE_ROWS)), row2(norm_final_g), tp, ts)


def kernel(x_prompt, x_sample, norm_mix_g, norm_ffn_g, norm_final_g, w_in, b_cell_gates, gmlp_ln_g, gmlp_w_s,
           gmlp_b_s, mlstm_conv_w, mlstm_head_g, w_out, w_router_group, b_router_group, w_router_expert,
           b_router_expert, w_exp_gate, w_exp_up, w_exp_down):
    bp, sp, d = x_prompt.shape
    bs, ss, _ = x_sample.shape
    seq_lens = [sp] * bp + [ss] * bs
    yp, ys = _layer(
        x_prompt.reshape(bp * sp, d), x_sample.reshape(bs * ss, d), seq_lens,
        norm_mix_g[0], norm_ffn_g[0], norm_final_g, w_in[0], b_cell_gates[0], gmlp_ln_g[0], gmlp_w_s[0],
        gmlp_b_s[0], mlstm_conv_w[0], mlstm_head_g[0], w_out[0], w_router_group[0], b_router_group[0],
        w_router_expert[0], b_router_expert[0], w_exp_gate[0], w_exp_up[0], w_exp_down[0])
    return yp.reshape(bp, sp, d), ys.reshape(bs, ss, d)
```

```python
import functools

import jax
import jax.numpy as jnp
import numpy as np
from jax import lax
from jax.experimental import pallas as pl
from jax.experimental.pallas import tpu as pltpu

F32 = jnp.float32
BF16 = jnp.bfloat16
U32 = jnp.uint32
HI_HALF = np.uint32(0xFFFF0000)

D_MODEL = 2048
HALF_D = D_MODEL // 2
CHUNK = 128
G_GROUPS = 8
G_GDIM = D_MODEL // G_GROUPS
M_HEADS = 8
M_HDIM = D_MODEL // M_HEADS
CONV_W = 5
N_GROUPS = 4
EXPERTS_PER_GROUP = 8
N_EXPERTS = N_GROUPS * EXPERTS_PER_GROUP
D_EXPERT = 1024
EPS = 1e-6
LANES = 128
SUBLANES = 8
N_CELL_GATES = 4 * M_HEADS
OFF_CG = 6 * D_MODEL
OFF_MERGE = OFF_CG + N_CELL_GATES

ZG_GU, ZG_GV = range(2)
ZP_Q, ZP_V = range(2)
ZS_O, ZS_MA, ZS_MB = range(3)
PROJ_SUB = 256

V_AUG = M_HDIM + LANES
VMEM_LIMIT = 56 * 1024 * 1024
EXPERT_BLOCK = 512
DISPATCH_ROWS = 1024
COMBINE_ROWS = 256


def _cparams(sem):
    return pltpu.CompilerParams(dimension_semantics=sem, vmem_limit_bytes=VMEM_LIMIT)


def _rms(x, g):
    ms = jnp.mean(x * x, axis=-1, keepdims=True)
    return x * lax.rsqrt(ms + EPS) * g


def _sigmoid(x):
    return 1.0 / (1.0 + jnp.exp(-x))


def _gelu_tanh(x):
    c = np.float32(np.sqrt(2.0 / np.pi))
    return 0.5 * x * (1.0 + jnp.tanh(c * (x + np.float32(0.044715) * (x * x * x))))


def _matmul_act(h_ref, w_ref, z_ref, act):
    for c in range(w_ref.shape[1] // PROJ_SUB):
        cs = slice(c * PROJ_SUB, (c + 1) * PROJ_SUB)
        acc = jnp.dot(h_ref[...], w_ref[:, cs], preferred_element_type=F32)
        z_ref[:, cs] = act(acc).astype(BF16)


def _proj_first_kernel(xp_ref, xs_ref, g_ref, w_ref, wkt_ref, wcg_ref, z_ref, kt_ref, cg_ref, h_ref,
                       *, n_prompt_tiles, n_q_steps):
    i = pl.program_id(0)
    j = pl.program_id(1)

    @pl.when(j == 0)
    def _():
        @pl.when(i < n_prompt_tiles)
        def _():
            h_ref[...] = _rms(xp_ref[...], g_ref[...]).astype(BF16)

        @pl.when(i >= n_prompt_tiles)
        def _():
            h_ref[...] = _rms(xs_ref[...], g_ref[...]).astype(BF16)

        cg_ref[...] = jnp.dot(h_ref[...], wcg_ref[...], preferred_element_type=F32)

    is_k = jnp.logical_and(j >= n_q_steps, j < 2 * n_q_steps)

    @pl.when(is_k)
    def _():
        kt_ref[...] = lax.dot_general(wkt_ref[...], h_ref[...], (((1,), (1,)), ((), ())),
                                      preferred_element_type=F32).astype(BF16)

    @pl.when(jnp.logical_not(is_k))
    def _():
        _matmul_act(h_ref, w_ref, z_ref, lambda a: a)


def _proj_first(xp, xs, g, w_qv, wkt, wcg, tm, tn):
    tp, ts = xp.shape[0], xs.shape[0]
    t = tp + ts
    npt, nst = tp // tm, ts // tm
    nq = D_MODEL // tn
    zcol = lambda j: jnp.where(j < nq, j, jnp.where(j < 2 * nq, nq - 1, j - nq))
    ktrow = lambda j: jnp.clip(j - nq, 0, nq - 1)
    return pl.pallas_call(
        functools.partial(_proj_first_kernel, n_prompt_tiles=npt, n_q_steps=nq),
        grid=(npt + nst, 3 * nq),
        in_specs=[
            pl.BlockSpec((tm, D_MODEL), lambda i, j: (jnp.minimum(i, npt - 1), 0), pipeline_mode=pl.Buffered(1)),
            pl.BlockSpec((tm, D_MODEL), lambda i, j: (jnp.maximum(i - npt, 0), 0), pipeline_mode=pl.Buffered(1)),
            pl.BlockSpec((1, D_MODEL), lambda i, j: (0, 0)),
            pl.BlockSpec((D_MODEL, tn), lambda i, j: (0, zcol(j))),
            pl.BlockSpec((tn, D_MODEL), lambda i, j: (ktrow(j), 0)),
            pl.BlockSpec((D_MODEL, LANES), lambda i, j: (0, 0)),
        ],
        out_specs=[
            pl.BlockSpec((tm, tn), lambda i, j: (i, zcol(j))),
            pl.BlockSpec((tn, tm), lambda i, j: (ktrow(j), i)),
            pl.BlockSpec((tm, LANES), lambda i, j: (i, 0)),
            pl.BlockSpec((tm, D_MODEL), lambda i, j: (i, 0)),
        ],
        out_shape=[
            jax.ShapeDtypeStruct((t, 2 * D_MODEL), BF16),
            jax.ShapeDtypeStruct((D_MODEL, t), BF16),
            jax.ShapeDtypeStruct((t, LANES), F32),
            jax.ShapeDtypeStruct((t, D_MODEL), BF16),
        ],
        compiler_params=_cparams(("arbitrary", "arbitrary")),
        name="proj_plain",
    )(xp, xs, g, w_qv, wkt, wcg)


def _proj_act_kernel(h_ref, w_ref, z_ref, *, act):
    _matmul_act(h_ref, w_ref, z_ref, act)


def _proj_act(h, w, act, name, tm, tn):
    t = h.shape[0]
    return pl.pallas_call(
        functools.partial(_proj_act_kernel, act=act),
        grid=(t // tm, w.shape[1] // tn),
        in_specs=[
            pl.BlockSpec((tm, D_MODEL), lambda i, j: (i, 0)),
            pl.BlockSpec((D_MODEL, tn), lambda i, j: (0, j)),
        ],
        out_specs=pl.BlockSpec((tm, tn), lambda i, j: (i, j)),
        out_shape=jax.ShapeDtypeStruct((t, w.shape[1]), BF16),
        compiler_params=_cparams(("arbitrary", "arbitrary")),
        name=name,
    )(h, w)


def _gmlp_kernel(gu_ref, gv_ref, ma_ref, lng_ref, ws_ref, bst_ref, out_ref, vn_ref, *, rows):
    v = gv_ref[...].astype(F32)
    mu = jnp.mean(v, axis=-1, keepdims=True)
    vc = v - mu
    var = jnp.mean(vc * vc, axis=-1, keepdims=True)
    vn_ref[...] = (vc * lax.rsqrt(var + EPS) * lng_ref[...]).astype(BF16)
    for c in range(rows // CHUNK):
        rs = slice(c * CHUNK, (c + 1) * CHUNK)
        for g in range(G_GROUPS):
            cs = slice(g * G_GDIM, (g + 1) * G_GDIM)
            s = jnp.dot(ws_ref[g], vn_ref[rs, cs], preferred_element_type=F32) + bst_ref[:, g:g + 1]
            a = gu_ref[rs, cs].astype(F32) * s
            out_ref[rs, cs] = (ma_ref[rs, cs].astype(F32) * a).astype(BF16)


def _gmlp(zg, zs, lng, ws, bst, rows):
    t = zg.shape[0]
    zspec = lambda b: pl.BlockSpec((rows, D_MODEL), lambda i, b=b: (i, b))
    return pl.pallas_call(
        functools.partial(_gmlp_kernel, rows=rows),
        grid=(t // rows,),
        in_specs=[
            zspec(ZG_GU), zspec(ZG_GV), zspec(ZS_MA),
            pl.BlockSpec((1, D_MODEL), lambda i: (0, 0)),
            pl.BlockSpec((G_GROUPS, CHUNK, CHUNK), lambda i: (0, 0, 0)),
            pl.BlockSpec((CHUNK, LANES), lambda i: (0, 0)),
        ],
        out_specs=pl.BlockSpec((rows, D_MODEL), lambda i: (i, 0)),
        out_shape=jax.ShapeDtypeStruct((t, D_MODEL), BF16),
        scratch_shapes=[pltpu.VMEM((rows, D_MODEL), BF16)],
        compiler_params=_cparams(("arbitrary",)),
        name="gmlp",
    )(zg, zg, zs, lng, ws, bst)


HALO = 16
CONV_CH = 16


def _conv_kernel(flags_ref, qprev_ref, qcur_ref, qnext_ref, wq_ref, kprev_ref, kcur_ref, knext_ref, wk_ref,
                 q_ref, kt_ref, *, rows, n_q_blocks):
    i = pl.program_id(0)
    j = pl.program_id(1)
    keep_prev = (flags_ref[i, 0] == 0).astype(F32)
    keep_next = (flags_ref[i, 1] == 0).astype(F32)
    half = CONV_W // 2

    @pl.when(j < n_q_blocks)
    def _():
        x = qcur_ref[...].astype(F32)
        p = qprev_ref[...].astype(F32) * keep_prev
        n = qnext_ref[...].astype(F32) * keep_next
        row = lax.broadcasted_iota(jnp.int32, x.shape, 0)
        w = wq_ref[...]
        acc = x * w[half:half + 1, :]
        for d in range(1, half + 1):
            xm = pltpu.roll(x, d, 0)
            for r in range(d):
                xm = jnp.where(row == r, p[HALO - d + r:HALO - d + r + 1, :], xm)
            acc = acc + xm * w[half - d:half - d + 1, :]
            xp = pltpu.roll(x, rows - d, 0)
            for r in range(d):
                xp = jnp.where(row == rows - d + r, n[r:r + 1, :], xp)
            acc = acc + xp * w[half + d:half + d + 1, :]
        q_ref[...] = (acc * _sigmoid(acc)).astype(BF16)

    @pl.when(j >= n_q_blocks)
    def _():
        n_lt = rows // LANES
        lane = lax.broadcasted_iota(jnp.int32, (CONV_CH, LANES), 1)

        def body(g, _):
            rs = pl.ds(pl.multiple_of(g * CONV_CH, CONV_CH), CONV_CH)
            tiles = ([kprev_ref[rs, :].astype(F32) * keep_prev]
                     + [kcur_ref[rs, lt * LANES:(lt + 1) * LANES].astype(F32) for lt in range(n_lt)]
                     + [knext_ref[rs, :].astype(F32) * keep_next])
            w = [wk_ref[tap, rs, :] for tap in range(CONV_W)]
            rolled = {}

            def rot(m, d):
                if (m, d) not in rolled:
                    rolled[(m, d)] = pltpu.roll(tiles[m], d % LANES, 1)
                return rolled[(m, d)]

            for lt in range(n_lt):
                c = lt + 1
                acc = tiles[c] * w[half]
                for d in range(1, half + 1):
                    before = jnp.where(lane >= d, rot(c, d), rot(c - 1, d))
                    after = jnp.where(lane < LANES - d, rot(c, -d), rot(c + 1, -d))
                    acc = acc + before * w[half - d] + after * w[half + d]
                y = acc * _sigmoid(acc) * np.float32(M_HDIM ** -0.5)
                kt_ref[rs, lt * LANES:(lt + 1) * LANES] = y.astype(BF16)
            return 0
        lax.fori_loop(0, kcur_ref.shape[0] // CONV_CH, body, 0, unroll=8)


def _conv(zp, kt_raw, w, flags, rows, cols):
    t = zp.shape[0]
    nqb = D_MODEL // cols
    hb = rows // HALO
    nhb = t // HALO
    lb = rows // LANES
    nlb = t // LANES
    w_q = jnp.pad(w[:, :D_MODEL], ((0, SUBLANES - CONV_W), (0, 0)))
    w_k = jnp.broadcast_to(w[:, D_MODEL:, None], (CONV_W, D_MODEL, LANES))
    qcol = lambda j: jnp.minimum(j, nqb - 1)
    krow = lambda j: jnp.maximum(j - nqb, 0)
    kern = functools.partial(_conv_kernel, rows=rows, n_q_blocks=nqb)
    grid_spec = pltpu.PrefetchScalarGridSpec(
        num_scalar_prefetch=1,
        grid=(t // rows, 2 * nqb),
        in_specs=[
            pl.BlockSpec((HALO, cols), lambda i, j, f: (jnp.maximum(i * hb - 1, 0), qcol(j))),
            pl.BlockSpec((rows, cols), lambda i, j, f: (i, qcol(j))),
            pl.BlockSpec((HALO, cols), lambda i, j, f: (jnp.minimum((i + 1) * hb, nhb - 1), qcol(j))),
            pl.BlockSpec((SUBLANES, cols), lambda i, j, f: (0, qcol(j))),
            pl.BlockSpec((cols, LANES), lambda i, j, f: (krow(j), jnp.maximum(i * lb - 1, 0))),
            pl.BlockSpec((cols, rows), lambda i, j, f: (krow(j), i)),
            pl.BlockSpec((cols, LANES), lambda i, j, f: (krow(j), jnp.minimum((i + 1) * lb, nlb - 1))),
            pl.BlockSpec((CONV_W, cols, LANES), lambda i, j, f: (0, krow(j), 0)),
        ],
        out_specs=[
            pl.BlockSpec((rows, cols), lambda i, j, f: (i, qcol(j))),
            pl.BlockSpec((cols, rows), lambda i, j, f: (krow(j), i)),
        ],
    )
    return pl.pallas_call(
        kern,
        grid_spec=grid_spec,
        out_shape=[jax.ShapeDtypeStruct((t, D_MODEL), BF16), jax.ShapeDtypeStruct((D_MODEL, t), BF16)],
        compiler_params=_cparams(("arbitrary", "arbitrary")),
        name="conv",
    )(flags, zp, zp, zp, w_q, kt_raw, kt_raw, kt_raw, w_k)


def _log_sigmoid(x):
    return jnp.minimum(x, 0.0) - jnp.log1p(jnp.exp(-jnp.abs(x)))


def _running_max_lanes(x, reverse):
    n = x.shape[1]
    lane = lax.broadcasted_iota(jnp.int32, x.shape, 1)
    shift = 1
    while shift < n:
        if reverse:
            moved = jnp.where(lane < n - shift, pltpu.roll(x, n - shift, 1), -jnp.inf)
        else:
            moved = jnp.where(lane >= shift, pltpu.roll(x, shift, 1), -jnp.inf)
        x = jnp.maximum(x, moved)
        shift *= 2
    return x


def _mlstm_kernel(reset_ref, order_ref, *refs, reverse):
    del order_ref
    if reverse:
        (q_ref, kt_ref, v_ref, cg_ref, bias_ref, ones_ref, hf_ref, o_ref, mb_ref, ga_ref, hg_ref,
         out_ref, c_ref, m_ref) = refs
    else:
        q_ref, kt_ref, v_ref, cg_ref, bias_ref, ones_ref, out_ref, c_ref, m_ref = refs
    step = pl.program_id(0)

    @pl.when(reset_ref[step] == 1)
    def _():
        c_ref[...] = jnp.zeros_like(c_ref)
        m_ref[...] = jnp.zeros_like(m_ref)

    L = CHUNK
    gates_t = (cg_ref[...] + bias_ref[...]).T
    lf_t = _log_sigmoid(gates_t)
    r_i = lax.broadcasted_iota(jnp.int32, (L, L), 0)
    c_i = lax.broadcasted_iota(jnp.int32, (L, L), 1)
    if reverse:
        valid = c_i >= r_i
        i_off, f_off, last = 2 * M_HEADS, 3 * M_HEADS, 0
    else:
        valid = c_i <= r_i
        i_off, f_off, last = 0, M_HEADS, L - 1
    tri = valid.astype(F32)
    f_rows = lax.dot_general(lf_t, tri, (((1,), (1,)), ((), ())), precision=lax.Precision.HIGHEST,
                             preferred_element_type=F32)[f_off:f_off + M_HEADS, :]
    b_rows = gates_t[i_off:i_off + M_HEADS, :] - f_rows
    cm_rows = _running_max_lanes(b_rows, reverse)
    cols = jnp.concatenate([cm_rows, f_rows, jnp.zeros((L - 2 * M_HEADS, L), F32)], axis=0).T
    heads = range(M_HEADS)
    hsl = [slice(h * M_HDIM, (h + 1) * M_HDIM) for h in heads]
    dot = functools.partial(jnp.dot, preferred_element_type=F32)
    q = [q_ref[:, hsl[h]] for h in heads]
    kt = [kt_ref[hsl[h], :] for h in heads]
    c_old = [c_ref[h] for h in heads]
    s = [dot(q[h], kt[h]) for h in heads]
    qc = [dot(q[h], c_old[h].astype(BF16)) for h in heads]
    ones_blk = ones_ref[...]
    v_aug =[jnp.concatenate([v_ref[:, hsl[h]], ones_blk], axis=1) for h in heads]
    brow = [b_rows[h:h + 1, :] for h in heads]
    f_last = [f_rows[h:h + 1, last:last + 1] for h in heads]
    m_old = [m_ref[h][:, 0:1] for h in heads]
    u = [jnp.maximum(m_old[h], cols[:, h:h + 1]) for h in heads]

    p =[(s[h] * jnp.exp(jnp.where(valid, brow[h] - u[h], -jnp.inf))).astype(BF16) for h in heads]
    nd = [jnp.exp(m_old[h] - u[h]) * qc[h] + dot(p[h], v_aug[h]) for h in heads]

    for h in heads:
        g_row = f_last[h] + brow[h]
        m_new = jnp.maximum(f_last[h] + m_old[h], jnp.max(g_row, axis=-1, keepdims=True))
        a = jnp.exp(f_last[h] + m_old[h] - m_new)
        kwt = (kt[h].astype(F32) * jnp.exp(g_row - m_new)).astype(BF16)
        c_ref[h] = a * c_old[h] + dot(kwt, v_aug[h])
        m_ref[h] = jnp.broadcast_to(m_new, (1, LANES))

    for h in heads:
        f_col = cols[:, M_HEADS + h:M_HEADS + h + 1]
        den = nd[h][:, M_HDIM:M_HDIM + 1]
        hout = nd[h][:, :M_HDIM] * (1.0 / jnp.maximum(jnp.abs(den), jnp.exp(-(f_col + u[h]))))
        if reverse:
            hsum = hf_ref[:, hsl[h]].astype(F32) + hout
            hn = _rms(hsum, hg_ref[h:h + 1, :])
            b = hn * o_ref[:, hsl[h]].astype(F32)
            merged = ga_ref[:, hsl[h]].astype(F32) + mb_ref[:, hsl[h]].astype(F32) * b
            out_ref[:, hsl[h]] = merged.astype(BF16)
        else:
            out_ref[:, hsl[h]] = hout.astype(BF16)


def _mlstm(qc, kt, zp, zs, cg, bias, reset, order, reverse, extras=None):
    t = zp.shape[0]
    nchunks = t // CHUNK
    row = lambda b: pl.BlockSpec((CHUNK, D_MODEL), lambda i, rs, od, b=b: (od[i], b))
    in_specs = [row(0),
                pl.BlockSpec((D_MODEL, CHUNK), lambda i, rs, od: (0, od[i])),
                row(ZP_V),
                pl.BlockSpec((CHUNK, LANES), lambda i, rs, od: (od[i], 0)),
                pl.BlockSpec((1, LANES), lambda i, rs, od: (0, 0)),
                pl.BlockSpec((CHUNK, LANES), lambda i, rs, od: (0, 0))]
    ones_col = jnp.zeros((CHUNK, LANES), BF16).at[:, 0].set(1)
    args = [qc, kt, zp, cg, bias, ones_col]
    if reverse:
        hf, ga, hg = extras
        in_specs += [row(0), row(ZS_O), row(ZS_MB), row(0),
                     pl.BlockSpec((M_HEADS, M_HDIM), lambda i, rs, od: (0, 0))]
        args += [hf, zs, zs, ga, hg]
    grid_spec = pltpu.PrefetchScalarGridSpec(
        num_scalar_prefetch=2,
        grid=(nchunks,),
        in_specs=in_specs,
        out_specs=row(0),
        scratch_shapes=[pltpu.VMEM((M_HEADS, M_HDIM, V_AUG), F32),
                        pltpu.VMEM((M_HEADS, 1, LANES), F32)],
    )
    return pl.pallas_call(
        functools.partial(_mlstm_kernel, reverse=reverse),
        grid_spec=grid_spec,
        out_shape=jax.ShapeDtypeStruct((t, D_MODEL), BF16),
        compiler_params=_cparams(("arbitrary",)),
        name="mlstm_bwd" if reverse else "mlstm_fwd",
    )(reset, order, *args)


def _pack_bf16_pair(lo, hi):
    lo_bits = pltpu.bitcast(lo.astype(F32), U32) >> 16
    hi_bits = pltpu.bitcast(hi.astype(F32), U32) & HI_HALF
    return hi_bits | lo_bits


def _unpack_bf16_pair(p):
    lo = pltpu.bitcast(p << 16, F32)
    hi = pltpu.bitcast(p & HI_HALF, F32)
    return lo.astype(BF16), hi.astype(BF16)


def _outproj_kernel(m_ref, xp_ref, xs_ref, w_ref, g_ref, wr_hi_ref, wr_lo_ref, br_ref, x1_ref, hp_ref, lg_ref,
                    *, n_prompt_tiles):
    i = pl.program_id(0)
    y = jnp.dot(m_ref[...], w_ref[...], preferred_element_type=F32)

    def finish(x):
        x1 = x + y
        x1_ref[...] = x1
        hn = _rms(x1, g_ref[...])
        h_hi = hn.astype(BF16)
        h_lo = (hn - h_hi.astype(F32)).astype(BF16)
        hp_ref[...] = _pack_bf16_pair(h_hi[:, :HALF_D], h_hi[:, HALF_D:])
        dot = functools.partial(jnp.dot, preferred_element_type=F32)
        lg_ref[...] = (dot(h_hi, wr_hi_ref[...]) + dot(h_hi, wr_lo_ref[...]) + dot(h_lo, wr_hi_ref[...])
                       + br_ref[...])

    @pl.when(i < n_prompt_tiles)
    def _():
        finish(xp_ref[...])

    @pl.when(i >= n_prompt_tiles)
    def _():
        finish(xs_ref[...])


def _outproj(merged, xp, xs, w, g, wr, br, tm):
    t = merged.shape[0]
    npt = xp.shape[0] // tm
    full = lambda shape: pl.BlockSpec(shape, lambda i: (0, 0))
    wr_hi = wr.astype(BF16)
    wr_lo = (wr - wr_hi.astype(F32)).astype(BF16)
    return pl.pallas_call(
        functools.partial(_outproj_kernel, n_prompt_tiles=npt),
        grid=(t // tm,),
        in_specs=[
            pl.BlockSpec((tm, D_MODEL), lambda i: (i, 0)),
            pl.BlockSpec((tm, D_MODEL), lambda i: (jnp.minimum(i, npt - 1), 0)),
            pl.BlockSpec((tm, D_MODEL), lambda i: (jnp.maximum(i - npt, 0), 0)),
            full((D_MODEL, D_MODEL)), full((1, D_MODEL)), full((D_MODEL, LANES)), full((D_MODEL, LANES)),
            full((1, LANES)),
        ],
        out_specs=[
            pl.BlockSpec((tm, D_MODEL), lambda i: (i, 0)),
            pl.BlockSpec((tm, HALF_D), lambda i: (i, 0)),
            pl.BlockSpec((tm, LANES), lambda i: (i, 0)),
        ],
        out_shape=[
            jax.ShapeDtypeStruct((t, D_MODEL), F32),
            jax.ShapeDtypeStruct((t, HALF_D), U32),
            jax.ShapeDtypeStruct((t, LANES), F32),
        ],
        compiler_params=_cparams(("arbitrary",)),
        name="outproj",
    )(merged, xp, xs, w, g, wr_hi, wr_lo, br)


R_W0, R_W1, R_E0, R_E1, R_RANK0, R_RANK1 = range(6)


def _route_kernel(lg_ref, rt_ref, cnt_ref, carry_ref, *, rows):
    i = pl.program_id(0)

    @pl.when(i == 0)
    def _():
        carry_ref[...] = jnp.zeros_like(carry_ref)

    lg = lg_ref[...]
    lane = lax.broadcasted_iota(jnp.int32, lg.shape, 1).astype(F32)
    big = np.float32(LANES)
    neg = -jnp.inf
    rmax = lambda a: jnp.max(a, axis=-1, keepdims=True)
    rmin = lambda a: jnp.min(a, axis=-1, keepdims=True)

    gmask = lane < N_GROUPS
    gl = jnp.where(gmask, lg, neg)
    gmax = rmax(gl)
    gsel = rmin(jnp.where(gl == gmax, lane, big))
    p_g = 1.0 / jnp.sum(jnp.where(gmask, jnp.exp(gl - gmax), 0.0), axis=-1, keepdims=True)

    lo = N_GROUPS + EXPERTS_PER_GROUP * gsel
    emask = jnp.logical_and(lane >= lo, lane < lo + EXPERTS_PER_GROUP)
    el = jnp.where(emask, lg, neg)
    v0 = rmax(el)
    i0 = rmin(jnp.where(el == v0, lane, big))
    el1 = jnp.where(lane == i0, neg, el)
    v1 = rmax(el1)
    i1 = rmin(jnp.where(el1 == v1, lane, big))
    ex = jnp.exp(v1 - v0)
    w0 = p_g / (1.0 + ex)
    w1 = p_g * ex / (1.0 + ex)
    e0 = i0 - N_GROUPS
    e1 = i1 - N_GROUPS

    oh0 = lane == e0
    oh1 = lane == e1
    onehot = jnp.logical_or(oh0, oh1)
    r_i = lax.broadcasted_iota(jnp.int32, (rows, rows), 0)
    c_i = lax.broadcasted_iota(jnp.int32, (rows, rows), 1)
    strict = (c_i < r_i).astype(BF16)
    before = jnp.dot(strict, onehot.astype(BF16), preferred_element_type=F32) + carry_ref[0:1, :]
    rank0 = jnp.sum(jnp.where(oh0, before, 0.0), axis=-1, keepdims=True)
    rank1 = jnp.sum(jnp.where(oh1, before, 0.0), axis=-1, keepdims=True)
    total = carry_ref[0:1, :] + jnp.sum(onehot.astype(F32), axis=0, keepdims=True)
    carry_ref[...] = jnp.broadcast_to(total, carry_ref.shape)
    cnt_ref[...] = jnp.broadcast_to(total, cnt_ref.shape)

    rec = jnp.zeros(lg.shape, F32)
    for idx, val in ((R_W0, w0), (R_W1, w1), (R_E0, e0), (R_E1, e1), (R_RANK0, rank0), (R_RANK1, rank1)):
        rec = jnp.where(lane == idx, val, rec)
    rt_ref[...] = rec


def _route(logits, rows):
    t = logits.shape[0]
    return pl.pallas_call(
        functools.partial(_route_kernel, rows=rows),
        grid=(t // rows,),
        in_specs=[pl.BlockSpec((rows, LANES), lambda i: (i, 0))],
        out_specs=[pl.BlockSpec((rows, LANES), lambda i: (i, 0)),
                   pl.BlockSpec((8, LANES), lambda i: (0, 0))],
        out_shape=[jax.ShapeDtypeStruct((t, LANES), F32),
                   jax.ShapeDtypeStruct((8, LANES), F32)],
        scratch_shapes=[pltpu.VMEM((8, LANES), F32)],
        compiler_params=_cparams(("arbitrary",)),
        name="route",
    )(logits)


def _dispatch_kernel(zstart_ref, zcount_ref, nused_ref, dest_ref, hp_ref, xs_hbm, zbuf, sem, zsem,
                     *, tm, bm, nblocks):
    i = pl.program_id(0)

    @pl.when(i == 0)
    def _():
        zbuf[...] = jnp.zeros_like(zbuf)

        def per_expert(e, _):
            start, n = zstart_ref[e], zcount_ref[e]

            def issue(r, _):
                pltpu.make_async_copy(zbuf.at[pl.ds(0, 1), :], xs_hbm.at[pl.ds(start + r, 1), :], zsem).start()
                return 0

            def wait(r, _):
                pltpu.make_async_copy(zbuf.at[pl.ds(0, 1), :], xs_hbm.at[pl.ds(start + r, 1), :], zsem).wait()
                return 0
            lax.fori_loop(0, n, issue, 0)
            lax.fori_loop(0, n, wait, 0)
            return 0
        lax.fori_loop(0, N_EXPERTS, per_expert, 0)

        def tail_copy(b):
            row0 = pl.multiple_of(b * bm, bm)
            return pltpu.make_async_copy(zbuf, xs_hbm.at[pl.ds(row0, bm), :], zsem)

        def tail_issue(b, _):
            tail_copy(b).start()
            return 0

        def tail_wait(b, _):
            tail_copy(b).wait()
            return 0
        lax.fori_loop(nused_ref[0], nblocks, tail_issue, 0)
        lax.fori_loop(nused_ref[0], nblocks, tail_wait, 0)

    def body(r, _):
        for k in range(2):
            pltpu.make_async_copy(hp_ref.at[pl.ds(r, 1), :], xs_hbm.at[pl.ds(dest_ref[0, k, r], 1), :],
                                  sem).start()
        return 0
    lax.fori_loop(0, tm, body, 0, unroll=8)
    for k in range(2):
        pltpu.make_async_copy(hp_ref, xs_hbm.at[pl.ds(0, tm), :], sem).wait()


def _dispatch(hp, dest3, zstart, zcount, n_used, bm, nblocks):
    ntiles, _, tm = dest3.shape
    grid_spec = pltpu.PrefetchScalarGridSpec(
        num_scalar_prefetch=3,
        grid=(ntiles,),
        in_specs=[
            pl.BlockSpec((1, 2, tm), lambda i, zs, zc, nu: (i, 0, 0), memory_space=pltpu.SMEM),
            pl.BlockSpec((tm, HALF_D), lambda i, zs, zc, nu: (i, 0)),
        ],
        out_specs=pl.BlockSpec(memory_space=pl.ANY),
        scratch_shapes=[
            pltpu.VMEM((bm, HALF_D), U32),
            pltpu.SemaphoreType.DMA(()),
            pltpu.SemaphoreType.DMA(()),
        ],
    )
    return pl.pallas_call(
        functools.partial(_dispatch_kernel, tm=tm, bm=bm, nblocks=nblocks),
        grid_spec=grid_spec,
        out_shape=jax.ShapeDtypeStruct((nblocks * bm, HALF_D), U32),
        compiler_params=_cparams(("arbitrary",)),
        name="dispatch",
    )(zstart, zcount, n_used, dest3, hp)


def _experts_kernel(be_ref, nused_ref, xs_ref, wg_ref, wu_ref, wd_ref, out_ref):
    del be_ref
    b = pl.program_id(0)

    @pl.when(b < nused_ref[0])
    def _():
        x_lo, x_hi = _unpack_bf16_pair(xs_ref[...])
        xb = jnp.concatenate([x_lo, x_hi], axis=1)
        hg = jnp.dot(xb, wg_ref[0], preferred_element_type=F32)
        hu = jnp.dot(xb, wu_ref[0], preferred_element_type=F32)
        hmid = (hg * _sigmoid(hg) * hu).astype(BF16)
        o = jnp.dot(hmid, wd_ref[0], preferred_element_type=F32).astype(BF16)
        out_ref[...] = _pack_bf16_pair(o[:, :HALF_D], o[:, HALF_D:])

    @pl.when(b >= nused_ref[0])
    def _():
        out_ref[...] = jnp.zeros_like(out_ref)


def _experts(xs, block_e, n_used, wg, wu, wd, bm):
    nblocks = block_e.shape[0]
    wspec = lambda shape: pl.BlockSpec((1,) + shape, lambda b, be, nu: (be[b], 0, 0))
    grid_spec = pltpu.PrefetchScalarGridSpec(
        num_scalar_prefetch=2,
        grid=(nblocks,),
        in_specs=[
            pl.BlockSpec((bm, HALF_D), lambda b, be, nu: (jnp.minimum(b, nu[0] - 1), 0)),
            wspec((D_MODEL, D_EXPERT)), wspec((D_MODEL, D_EXPERT)), wspec((D_EXPERT, D_MODEL)),
        ],
        out_specs=pl.BlockSpec((bm, HALF_D), lambda b, be, nu: (b, 0)),
    )
    return pl.pallas_call(
        _experts_kernel,
        grid_spec=grid_spec,
        out_shape=jax.ShapeDtypeStruct((nblocks * bm, HALF_D), U32),
        compiler_params=_cparams(("arbitrary",)),
        name="experts",
    )(block_e, n_used, xs, wg, wu, wd)


def _combine_kernel(dcur_ref, dnext_ref, x1_ref, rt_ref, g_ref, eo_hbm, yp_ref, ys_ref, gbuf, gsem,
                    *, tm, n_prompt_tiles, n_tiles):
    i = pl.program_id(0)
    slot = lax.rem(i, 2)

    def start_gather(idx_ref, sl):
        def body(r, _):
            for k in range(2):
                pltpu.make_async_copy(eo_hbm.at[pl.ds(idx_ref[0, k, r], 1), :], gbuf.at[sl, k, pl.ds(r, 1), :],
                                      gsem.at[sl]).start()
            return 0
        lax.fori_loop(0, tm, body, 0, unroll=8)

    @pl.when(i == 0)
    def _():
        start_gather(dcur_ref, 0)

    @pl.when(i + 1 < n_tiles)
    def _():
        start_gather(dnext_ref, 1 - slot)

    pltpu.make_async_copy(gbuf.at[slot], gbuf.at[slot], gsem.at[slot]).wait()
    rt = rt_ref[...]
    w0 = rt[:, R_W0:R_W0 + 1]
    w1 = rt[:, R_W1:R_W1 + 1]
    lo0, hi0 = _unpack_bf16_pair(gbuf[slot, 0])
    lo1, hi1 = _unpack_bf16_pair(gbuf[slot, 1])
    y_lo = x1_ref[:, :HALF_D] + w0 * lo0.astype(F32) + w1 * lo1.astype(F32)
    y_hi = x1_ref[:, HALF_D:] + w0 * hi0.astype(F32) + w1 * hi1.astype(F32)
    ms = (jnp.sum(y_lo * y_lo, axis=-1, keepdims=True) + jnp.sum(y_hi * y_hi, axis=-1, keepdims=True)) / D_MODEL
    scale = lax.rsqrt(ms + EPS)

    def store(y_ref):
        y_ref[:, :HALF_D] = y_lo * scale * g_ref[:, :HALF_D]
        y_ref[:, HALF_D:] = y_hi * scale * g_ref[:, HALF_D:]

    @pl.when(i < n_prompt_tiles)
    def _():
        store(yp_ref)

    @pl.when(i >= n_prompt_tiles)
    def _():
        store(ys_ref)


def _combine(x1, route, eo, dest3, g, tp, ts):
    t = tp + ts
    ntiles, _, tm = dest3.shape
    npt = tp // tm
    return pl.pallas_call(
        functools.partial(_combine_kernel, tm=tm, n_prompt_tiles=npt, n_tiles=ntiles),
        grid=(ntiles,),
        in_specs=[
            pl.BlockSpec((1, 2, tm), lambda i: (i, 0, 0), memory_space=pltpu.SMEM),
            pl.BlockSpec((1, 2, tm), lambda i: (jnp.minimum(i + 1, ntiles - 1), 0, 0), memory_space=pltpu.SMEM),
            pl.BlockSpec((tm, D_MODEL), lambda i: (i, 0)),
            pl.BlockSpec((tm, LANES), lambda i: (i, 0)),
            pl.BlockSpec((1, D_MODEL), lambda i: (0, 0)),
            pl.BlockSpec(memory_space=pl.ANY),
        ],
        out_specs=[
            pl.BlockSpec((tm, D_MODEL), lambda i: (jnp.minimum(i, npt - 1), 0)),
            pl.BlockSpec((tm, D_MODEL), lambda i: (jnp.maximum(i - npt, 0), 0)),
        ],
        out_shape=[jax.ShapeDtypeStruct((tp, D_MODEL), F32),
                   jax.ShapeDtypeStruct((ts, D_MODEL), F32)],
        scratch_shapes=[pltpu.VMEM((2, 2, tm, HALF_D), U32), pltpu.SemaphoreType.DMA((2,))],
        compiler_params=_cparams(("arbitrary",)),
        name="combine",
    )(dest3, dest3, x1, route, g, eo)


def _tile(n, pref):
    t = min(pref, n)
    assert n % t == 0, (n, t)
    return t


def _layer(xp, xs, seq_lens, norm_mix_g, norm_ffn_g, norm_final_g, w_in, b_cg, gmlp_ln_g, gmlp_w_s,
           gmlp_b_s, conv_w, head_g, w_out, w_rg, b_rg, w_re, b_re, w_eg, w_eu, w_ed):
    tp, ts = xp.shape[0], xs.shape[0]
    t = tp + ts
    row2 = lambda a: a.reshape(1, -1).astype(F32)

    w_gelu = w_in[:, :2 * D_MODEL].astype(BF16)
    w_qv = jnp.concatenate([w_in[:, 2 * D_MODEL:3 * D_MODEL], w_in[:, 4 * D_MODEL:5 * D_MODEL]],
                           axis=1).astype(BF16)
    w_kt = w_in[:, 3 * D_MODEL:4 * D_MODEL].T.astype(BF16)
    w_sig = jnp.concatenate([w_in[:, 5 * D_MODEL:OFF_CG], w_in[:, OFF_MERGE:]], axis=1).astype(BF16)
    w_cg = jnp.pad(w_in[:, OFF_CG:OFF_MERGE], ((0, 0), (0, LANES - N_CELL_GATES))).astype(BF16)
    bias_cg = jnp.pad(b_cg.astype(F32), (0, LANES - N_CELL_GATES)).reshape(1, LANES)
    bst = jnp.pad(gmlp_b_s.astype(F32).T, ((0, 0), (0, LANES - G_GROUPS)))
    w_router = jnp.pad(jnp.concatenate([w_rg, w_re], axis=1).astype(F32),
                       ((0, 0), (0, LANES - N_GROUPS - N_EXPERTS)))
    b_router = jnp.pad(jnp.concatenate([b_rg, b_re]).astype(F32),
                       (0, LANES - N_GROUPS - N_EXPERTS)).reshape(1, LANES)

    tm = _tile(np.gcd(tp, ts), 1024)
    conv_rows = _tile(np.gcd.reduce(seq_lens), 512)
    starts = np.cumsum([0] + list(seq_lens))
    n_ct = t // conv_rows
    flags = np.zeros((n_ct, 2), np.int32)
    for s0, s1 in zip(starts[:-1], starts[1:]):
        flags[s0 // conv_rows, 0] = 1
        flags[s1 // conv_rows - 1, 1] = 1
    nchunks = t // CHUNK
    first = np.zeros((nchunks,), np.int32)
    last = np.zeros((nchunks,), np.int32)
    for s0, s1 in zip(starts[:-1], starts[1:]):
        first[s0 // CHUNK] = 1
        last[s1 // CHUNK - 1] = 1
    fwd_order = np.arange(nchunks, dtype=np.int32)
    bwd_order = fwd_order[::-1].copy()

    zp, kt_raw, cg, h = _proj_first(xp, xs, row2(norm_mix_g), w_qv, w_kt, w_cg, tm, 1024)
    zg = _proj_act(h, w_gelu, _gelu_tanh, "proj_gelu", tm, 2048)
    zs = _proj_act(h, w_sig, _sigmoid, "proj_sigmoid", tm, 2048)
    ga = _gmlp(zg, zs, row2(gmlp_ln_g), gmlp_w_s.astype(BF16), bst, _tile(np.gcd(tp, ts), 512))
    qc, kt = _conv(zp, kt_raw, conv_w.astype(F32), jnp.asarray(flags), conv_rows, 1024)
    hf = _mlstm(qc, kt, zp, zs, cg, bias_cg, jnp.asarray(first), jnp.asarray(fwd_order), reverse=False)
    merged = _mlstm(qc, kt, zp, zs, cg, bias_cg, jnp.asarray(last[::-1].copy()), jnp.asarray(bwd_order),
                    reverse=True, extras=(hf, ga, head_g.astype(F32)))
    x1, hp, logits = _outproj(merged, xp, xs, w_out.astype(BF16), row2(norm_ffn_g), w_router, b_router,
                              _tile(np.gcd(tp, ts), 512))

    route, counts = _route(logits, _tile(t, 512))
    bm = EXPERT_BLOCK
    nblocks = (2 * t) // bm + N_EXPERTS
    counts = counts[0, :N_EXPERTS].astype(jnp.int32)
    padded = ((counts + bm - 1) // bm) * bm
    pad_end = jnp.cumsum(padded)
    pad_start = pad_end - padded
    e_id = route[:, R_E0:R_E1 + 1].astype(jnp.int32)
    rank = route[:, R_RANK0:R_RANK1 + 1].astype(jnp.int32)
    e_iota = jnp.arange(N_EXPERTS, dtype=jnp.int32)
    start_of = jnp.sum(jnp.where(e_id[:, :, None] == e_iota, pad_start, 0), axis=-1)
    dest = start_of + rank
    block_start = jnp.arange(nblocks, dtype=jnp.int32) * bm
    n_used = (pad_end[-1] // bm).astype(jnp.int32).reshape(1)
    last_start = pad_end[-1] - bm
    block_e = jnp.sum(jnp.minimum(block_start, last_start)[:, None] >= pad_end[None, :], axis=1).astype(jnp.int32)
    tiles = lambda tm: dest.reshape(t // tm, tm, 2).transpose(0, 2, 1)
    xs = _dispatch(hp, tiles(_tile(np.gcd(tp, ts), DISPATCH_ROWS)), pad_start + counts, padded - counts,
                   n_used, bm, nblocks)
    eo = _experts(xs, block_e, n_used, w_eg.astype(BF16), w_eu.astype(BF16), w_ed.astype(BF16), bm)
    return _combine(x1, route, eo, tiles(_tile(np.gcd(tp, ts), COMBINE_ROWS)), row2(norm_final_g), tp, ts)


def kernel(x_prompt, x_sample, norm_mix_g, norm_ffn_g, norm_final_g, w_in, b_cell_gates, gmlp_ln_g, gmlp_w_s,
           gmlp_b_s, mlstm_conv_w, mlstm_head_g, w_out, w_router_group, b_router_group, w_router_expert,
           b_router_expert, w_exp_gate, w_exp_up, w_exp_down):
    bp, sp, d = x_prompt.shape
    bs, ss, _ = x_sample.shape
    seq_lens = [sp] * bp + [ss] * bs
    yp, ys = _layer(
        x_prompt.reshape(bp * sp, d), x_sample.reshape(bs * ss, d), seq_lens,
        norm_mix_g[0], norm_ffn_g[0], norm_final_g, w_in[0], b_cell_gates[0], gmlp_ln_g[0], gmlp_w_s[0],
        gmlp_b_s[0], mlstm_conv_w[0], mlstm_head_g[0], w_out[0], w_router_group[0], b_router_group[0],
        w_router_expert[0], b_router_expert[0], w_exp_gate[0], w_exp_up[0], w_exp_down[0])
    return yp.reshape(bp, sp, d), ys.reshape(bs, ss, d)
```

```python
import functools

import jax
import jax.numpy as jnp
import numpy as np
from jax import lax
from jax.experimental import pallas as pl
from jax.experimental.pallas import tpu as pltpu

F32 = jnp.float32
BF16 = jnp.bfloat16
U32 = jnp.uint32
HI_HALF = np.uint32(0xFFFF0000)

D_MODEL = 2048
HALF_D = D_MODEL // 2
CHUNK = 128
G_GROUPS = 8
G_GDIM = D_MODEL // G_GROUPS
M_HEADS = 8
M_HDIM = D_MODEL // M_HEADS
CONV_W = 5
N_GROUPS = 4
EXPERTS_PER_GROUP = 8
N_EXPERTS = N_GROUPS * EXPERTS_PER_GROUP
D_EXPERT = 1024
EPS = 1e-6
LANES = 128
SUBLANES = 8
N_CELL_GATES = 4 * M_HEADS
OFF_CG = 6 * D_MODEL
OFF_MERGE = OFF_CG + N_CELL_GATES

ZG_GU, ZG_GV = range(2)
ZP_Q, ZP_V = range(2)
ZS_O, ZS_MA, ZS_MB = range(3)
PROJ_SUB = 256

V_AUG = M_HDIM + LANES
VMEM_LIMIT = 56 * 1024 * 1024
EXPERT_BLOCK = 512
DISPATCH_ROWS = 1024
COMBINE_ROWS = 256


def _cparams(sem):
    return pltpu.CompilerParams(dimension_semantics=sem, vmem_limit_bytes=VMEM_LIMIT)


def _rms(x, g):
    ms = jnp.mean(x * x, axis=-1, keepdims=True)
    return x * lax.rsqrt(ms + EPS) * g


def _sigmoid(x):
    return 1.0 / (1.0 + jnp.exp(-x))


def _gelu_tanh(x):
    c = np.float32(np.sqrt(2.0 / np.pi))
    return 0.5 * x * (1.0 + jnp.tanh(c * (x + np.float32(0.044715) * (x * x * x))))


def _matmul_act(h_ref, w_ref, z_ref, act):
    for c in range(w_ref.shape[1] // PROJ_SUB):
        cs = slice(c * PROJ_SUB, (c + 1) * PROJ_SUB)
        acc = jnp.dot(h_ref[...], w_ref[:, cs], preferred_element_type=F32)
        z_ref[:, cs] = act(acc).astype(BF16)


def _proj_first_kernel(xp_ref, xs_ref, g_ref, w_ref, wcg_ref, z_ref, cg_ref, h_ref, *, n_prompt_tiles, act):
    i = pl.program_id(0)

    @pl.when(pl.program_id(1) == 0)
    def _():
        @pl.when(i < n_prompt_tiles)
        def _():
            h_ref[...] = _rms(xp_ref[...], g_ref[...]).astype(BF16)

        @pl.when(i >= n_prompt_tiles)
        def _():
            h_ref[...] = _rms(xs_ref[...], g_ref[...]).astype(BF16)

        cg_ref[...] = jnp.dot(h_ref[...], wcg_ref[...], preferred_element_type=F32)

    _matmul_act(h_ref, w_ref, z_ref, act)


def _proj_first(xp, xs, g, w, wcg, act, name, tm, tn):
    tp, ts = xp.shape[0], xs.shape[0]
    t = tp + ts
    npt, nst = tp // tm, ts // tm
    return pl.pallas_call(
        functools.partial(_proj_first_kernel, n_prompt_tiles=npt, act=act),
        grid=(npt + nst, w.shape[1] // tn),
        in_specs=[
            pl.BlockSpec((tm, D_MODEL), lambda i, j: (jnp.minimum(i, npt - 1), 0)),
            pl.BlockSpec((tm, D_MODEL), lambda i, j: (jnp.maximum(i - npt, 0), 0)),
            pl.BlockSpec((1, D_MODEL), lambda i, j: (0, 0)),
            pl.BlockSpec((D_MODEL, tn), lambda i, j: (0, j)),
            pl.BlockSpec((D_MODEL, LANES), lambda i, j: (0, 0)),
        ],
        out_specs=[
            pl.BlockSpec((tm, tn), lambda i, j: (i, j)),
            pl.BlockSpec((tm, LANES), lambda i, j: (i, 0)),
            pl.BlockSpec((tm, D_MODEL), lambda i, j: (i, 0)),
        ],
        out_shape=[
            jax.ShapeDtypeStruct((t, w.shape[1]), BF16),
            jax.ShapeDtypeStruct((t, LANES), F32),
            jax.ShapeDtypeStruct((t, D_MODEL), BF16),
        ],
        compiler_params=_cparams(("arbitrary", "arbitrary")),
        name=name,
    )(xp, xs, g, w, wcg)


def _proj_qkv_kernel(h_ref, w_ref, wkt_ref, z_ref, kt_ref, *, n_q_steps):
    j = pl.program_id(1)
    is_k = jnp.logical_and(j >= n_q_steps, j < 2 * n_q_steps)

    @pl.when(is_k)
    def _():
        kt_ref[...] = lax.dot_general(wkt_ref[...], h_ref[...], (((1,), (1,)), ((), ())),
                                      preferred_element_type=F32).astype(BF16)

    @pl.when(jnp.logical_not(is_k))
    def _():
        _matmul_act(h_ref, w_ref, z_ref, lambda a: a)


def _proj_qkv(h, w_qv, wkt, tm, tn):
    t = h.shape[0]
    nq = D_MODEL // tn
    zcol = lambda j: jnp.where(j < nq, j, jnp.where(j < 2 * nq, nq - 1, j - nq))
    ktrow = lambda j: jnp.clip(j - nq, 0, nq - 1)
    return pl.pallas_call(
        functools.partial(_proj_qkv_kernel, n_q_steps=nq),
        grid=(t // tm, 3 * nq),
        in_specs=[
            pl.BlockSpec((tm, D_MODEL), lambda i, j: (i, 0)),
            pl.BlockSpec((D_MODEL, tn), lambda i, j: (0, zcol(j))),
            pl.BlockSpec((tn, D_MODEL), lambda i, j: (ktrow(j), 0)),
        ],
        out_specs=[
            pl.BlockSpec((tm, tn), lambda i, j: (i, zcol(j))),
            pl.BlockSpec((tn, tm), lambda i, j: (ktrow(j), i)),
        ],
        out_shape=[
            jax.ShapeDtypeStruct((t, 2 * D_MODEL), BF16),
            jax.ShapeDtypeStruct((D_MODEL, t), BF16),
        ],
        compiler_params=_cparams(("arbitrary", "arbitrary")),
        name="proj_qkv",
    )(h, w_qv, wkt)


def _proj_act_kernel(h_ref, w_ref, z_ref, *, act):
    _matmul_act(h_ref, w_ref, z_ref, act)


def _proj_act(h, w, act, name, tm, tn):
    t = h.shape[0]
    return pl.pallas_call(
        functools.partial(_proj_act_kernel, act=act),
        grid=(t // tm, w.shape[1] // tn),
        in_specs=[
            pl.BlockSpec((tm, D_MODEL), lambda i, j: (i, 0)),
            pl.BlockSpec((D_MODEL, tn), lambda i, j: (0, j)),
        ],
        out_specs=pl.BlockSpec((tm, tn), lambda i, j: (i, j)),
        out_shape=jax.ShapeDtypeStruct((t, w.shape[1]), BF16),
        compiler_params=_cparams(("arbitrary", "arbitrary")),
        name=name,
    )(h, w)


def _gmlp_kernel(gu_ref, gv_ref, ma_ref, lng_ref, ws_ref, bst_ref, out_ref, vn_ref, *, rows):
    v = gv_ref[...].astype(F32)
    mu = jnp.mean(v, axis=-1, keepdims=True)
    vc = v - mu
    var = jnp.mean(vc * vc, axis=-1, keepdims=True)
    vn_ref[...] = (vc * lax.rsqrt(var + EPS) * lng_ref[...]).astype(BF16)
    for c in range(rows // CHUNK):
        rs = slice(c * CHUNK, (c + 1) * CHUNK)
        for g in range(G_GROUPS):
            cs = slice(g * G_GDIM, (g + 1) * G_GDIM)
            s = jnp.dot(ws_ref[g], vn_ref[rs, cs], preferred_element_type=F32) + bst_ref[:, g:g + 1]
            a = gu_ref[rs, cs].astype(F32) * s
            out_ref[rs, cs] = (ma_ref[rs, cs].astype(F32) * a).astype(BF16)


def _gmlp(zg, zs, lng, ws, bst, rows):
    t = zg.shape[0]
    zspec = lambda b: pl.BlockSpec((rows, D_MODEL), lambda i, b=b: (i, b))
    return pl.pallas_call(
        functools.partial(_gmlp_kernel, rows=rows),
        grid=(t // rows,),
        in_specs=[
            zspec(ZG_GU), zspec(ZG_GV), zspec(ZS_MA),
            pl.BlockSpec((1, D_MODEL), lambda i: (0, 0)),
            pl.BlockSpec((G_GROUPS, CHUNK, CHUNK), lambda i: (0, 0, 0)),
            pl.BlockSpec((CHUNK, LANES), lambda i: (0, 0)),
        ],
        out_specs=pl.BlockSpec((rows, D_MODEL), lambda i: (i, 0)),
        out_shape=jax.ShapeDtypeStruct((t, D_MODEL), BF16),
        scratch_shapes=[pltpu.VMEM((rows, D_MODEL), BF16)],
        compiler_params=_cparams(("arbitrary",)),
        name="gmlp",
    )(zg, zg, zs, lng, ws, bst)


HALO = 16
CONV_CH = 16


def _conv_kernel(flags_ref, qprev_ref, qcur_ref, qnext_ref, wq_ref, kprev_ref, kcur_ref, knext_ref, wk_ref,
                 q_ref, kt_ref, *, rows, n_q_blocks):
    i = pl.program_id(0)
    j = pl.program_id(1)
    keep_prev = (flags_ref[i, 0] == 0).astype(F32)
    keep_next = (flags_ref[i, 1] == 0).astype(F32)
    half = CONV_W // 2

    @pl.when(j < n_q_blocks)
    def _():
        x = qcur_ref[...].astype(F32)
        p = qprev_ref[...].astype(F32) * keep_prev
        n = qnext_ref[...].astype(F32) * keep_next
        row = lax.broadcasted_iota(jnp.int32, x.shape, 0)
        w = wq_ref[...]
        acc = x * w[half:half + 1, :]
        for d in range(1, half + 1):
            xm = pltpu.roll(x, d, 0)
            for r in range(d):
                xm = jnp.where(row == r, p[HALO - d + r:HALO - d + r + 1, :], xm)
            acc = acc + xm * w[half - d:half - d + 1, :]
            xp = pltpu.roll(x, rows - d, 0)
            for r in range(d):
                xp = jnp.where(row == rows - d + r, n[r:r + 1, :], xp)
            acc = acc + xp * w[half + d:half + d + 1, :]
        q_ref[...] = (acc * _sigmoid(acc)).astype(BF16)

    @pl.when(j >= n_q_blocks)
    def _():
        n_lt = rows // LANES
        lane = lax.broadcasted_iota(jnp.int32, (CONV_CH, LANES), 1)

        def body(g, _):
            rs = pl.ds(pl.multiple_of(g * CONV_CH, CONV_CH), CONV_CH)
            tiles = ([kprev_ref[rs, :].astype(F32) * keep_prev]
                     + [kcur_ref[rs, lt * LANES:(lt + 1) * LANES].astype(F32) for lt in range(n_lt)]
                     + [knext_ref[rs, :].astype(F32) * keep_next])
            w = [wk_ref[tap, rs, :] for tap in range(CONV_W)]
            rolled = {}

            def rot(m, d):
                if (m, d) not in rolled:
                    rolled[(m, d)] = pltpu.roll(tiles[m], d % LANES, 1)
                return rolled[(m, d)]

            for lt in range(n_lt):
                c = lt + 1
                acc = tiles[c] * w[half]
                for d in range(1, half + 1):
                    before = jnp.where(lane >= d, rot(c, d), rot(c - 1, d))
                    after = jnp.where(lane < LANES - d, rot(c, -d), rot(c + 1, -d))
                    acc = acc + before * w[half - d] + after * w[half + d]
                y = acc * _sigmoid(acc) * np.float32(M_HDIM ** -0.5)
                kt_ref[rs, lt * LANES:(lt + 1) * LANES] = y.astype(BF16)
            return 0
        lax.fori_loop(0, kcur_ref.shape[0] // CONV_CH, body, 0, unroll=8)


def _conv(zp, kt_raw, w, flags, rows, cols):
    t = zp.shape[0]
    nqb = D_MODEL // cols
    hb = rows // HALO
    nhb = t // HALO
    lb = rows // LANES
    nlb = t // LANES
    w_q = jnp.pad(w[:, :D_MODEL], ((0, SUBLANES - CONV_W), (0, 0)))
    w_k = jnp.broadcast_to(w[:, D_MODEL:, None], (CONV_W, D_MODEL, LANES))
    qcol = lambda j: jnp.minimum(j, nqb - 1)
    krow = lambda j: jnp.maximum(j - nqb, 0)
    kern = functools.partial(_conv_kernel, rows=rows, n_q_blocks=nqb)
    grid_spec = pltpu.PrefetchScalarGridSpec(
        num_scalar_prefetch=1,
        grid=(t // rows, 2 * nqb),
        in_specs=[
            pl.BlockSpec((HALO, cols), lambda i, j, f: (jnp.maximum(i * hb - 1, 0), qcol(j))),
            pl.BlockSpec((rows, cols), lambda i, j, f: (i, qcol(j))),
            pl.BlockSpec((HALO, cols), lambda i, j, f: (jnp.minimum((i + 1) * hb, nhb - 1), qcol(j))),
            pl.BlockSpec((SUBLANES, cols), lambda i, j, f: (0, qcol(j))),
            pl.BlockSpec((cols, LANES), lambda i, j, f: (krow(j), jnp.maximum(i * lb - 1, 0))),
            pl.BlockSpec((cols, rows), lambda i, j, f: (krow(j), i)),
            pl.BlockSpec((cols, LANES), lambda i, j, f: (krow(j), jnp.minimum((i + 1) * lb, nlb - 1))),
            pl.BlockSpec((CONV_W, cols, LANES), lambda i, j, f: (0, krow(j), 0)),
        ],
        out_specs=[
            pl.BlockSpec((rows, cols), lambda i, j, f: (i, qcol(j))),
            pl.BlockSpec((cols, rows), lambda i, j, f: (krow(j), i)),
        ],
    )
    return pl.pallas_call(
        kern,
        grid_spec=grid_spec,
        out_shape=[jax.ShapeDtypeStruct((t, D_MODEL), BF16), jax.ShapeDtypeStruct((D_MODEL, t), BF16)],
        compiler_params=_cparams(("arbitrary", "arbitrary")),
        name="conv",
    )(flags, zp, zp, zp, w_q, kt_raw, kt_raw, kt_raw, w_k)


def _log_sigmoid(x):
    return jnp.minimum(x, 0.0) - jnp.log1p(jnp.exp(-jnp.abs(x)))


def _running_max_lanes(x, reverse):
    n = x.shape[1]
    lane = lax.broadcasted_iota(jnp.int32, x.shape, 1)
    shift = 1
    while shift < n:
        if reverse:
            moved = jnp.where(lane < n - shift, pltpu.roll(x, n - shift, 1), -jnp.inf)
        else:
            moved = jnp.where(lane >= shift, pltpu.roll(x, shift, 1), -jnp.inf)
        x = jnp.maximum(x, moved)
        shift *= 2
    return x


def _mlstm_kernel(reset_ref, order_ref, *refs, reverse):
    del order_ref
    if reverse:
        (q_ref, kt_ref, v_ref, cg_ref, bias_ref, ones_ref, hf_ref, o_ref, mb_ref, ga_ref, hg_ref,
         out_ref, c_ref, m_ref) = refs
    else:
        q_ref, kt_ref, v_ref, cg_ref, bias_ref, ones_ref, out_ref, c_ref, m_ref = refs
    step = pl.program_id(0)

    @pl.when(reset_ref[step] == 1)
    def _():
        c_ref[...] = jnp.zeros_like(c_ref)
        m_ref[...] = jnp.zeros_like(m_ref)

    L = CHUNK
    gates_t = (cg_ref[...] + bias_ref[...]).T
    lf_t = _log_sigmoid(gates_t)
    r_i = lax.broadcasted_iota(jnp.int32, (L, L), 0)
    c_i = lax.broadcasted_iota(jnp.int32, (L, L), 1)
    if reverse:
        valid = c_i >= r_i
        i_off, f_off, last = 2 * M_HEADS, 3 * M_HEADS, 0
    else:
        valid = c_i <= r_i
        i_off, f_off, last = 0, M_HEADS, L - 1
    tri = valid.astype(F32)
    f_rows = lax.dot_general(lf_t, tri, (((1,), (1,)), ((), ())), precision=lax.Precision.HIGHEST,
                             preferred_element_type=F32)[f_off:f_off + M_HEADS, :]
    b_rows = gates_t[i_off:i_off + M_HEADS, :] - f_rows
    cm_rows = _running_max_lanes(b_rows, reverse)
    cols = jnp.concatenate([cm_rows, f_rows, jnp.zeros((L - 2 * M_HEADS, L), F32)], axis=0).T
    heads = range(M_HEADS)
    hsl = [slice(h * M_HDIM, (h + 1) * M_HDIM) for h in heads]
    dot = functools.partial(jnp.dot, preferred_element_type=F32)
    q = [q_ref[:, hsl[h]] for h in heads]
    kt = [kt_ref[hsl[h], :] for h in heads]
    c_old = [c_ref[h] for h in heads]
    s = [dot(q[h], kt[h]) for h in heads]
    qc = [dot(q[h], c_old[h].astype(BF16)) for h in heads]
    ones_blk = ones_ref[...]
    v_aug =[jnp.concatenate([v_ref[:, hsl[h]], ones_blk], axis=1) for h in heads]
    brow = [b_rows[h:h + 1, :] for h in heads]
    f_last = [f_rows[h:h + 1, last:last + 1] for h in heads]
    m_old = [m_ref[h][:, 0:1] for h in heads]
    u = [jnp.maximum(m_old[h], cols[:, h:h + 1]) for h in heads]

    p =[(s[h] * jnp.exp(jnp.where(valid, brow[h] - u[h], -jnp.inf))).astype(BF16) for h in heads]
    nd = [jnp.exp(m_old[h] - u[h]) * qc[h] + dot(p[h], v_aug[h]) for h in heads]

    for h in heads:
        g_row = f_last[h] + brow[h]
        m_new = jnp.maximum(f_last[h] + m_old[h], jnp.max(g_row, axis=-1, keepdims=True))
        a = jnp.exp(f_last[h] + m_old[h] - m_new)
        kwt = (kt[h].astype(F32) * jnp.exp(g_row - m_new)).astype(BF16)
        c_ref[h] = a * c_old[h] + dot(kwt, v_aug[h])
        m_ref[h] = jnp.broadcast_to(m_new, (1, LANES))

    for h in heads:
        f_col = cols[:, M_HEADS + h:M_HEADS + h + 1]
        den = nd[h][:, M_HDIM:M_HDIM + 1]
        hout = nd[h][:, :M_HDIM] * (1.0 / jnp.maximum(jnp.abs(den), jnp.exp(-(f_col + u[h]))))
        if reverse:
            hsum = hf_ref[:, hsl[h]].astype(F32) + hout
            hn = _rms(hsum, hg_ref[h:h + 1, :])
            b = hn * o_ref[:, hsl[h]].astype(F32)
            merged = ga_ref[:, hsl[h]].astype(F32) + mb_ref[:, hsl[h]].astype(F32) * b
            out_ref[:, hsl[h]] = merged.astype(BF16)
        else:
            out_ref[:, hsl[h]] = hout.astype(BF16)


def _mlstm(qc, kt, zp, zs, cg, bias, reset, order, reverse, extras=None):
    t = zp.shape[0]
    nchunks = t // CHUNK
    row = lambda b: pl.BlockSpec((CHUNK, D_MODEL), lambda i, rs, od, b=b: (od[i], b))
    in_specs = [row(0),
                pl.BlockSpec((D_MODEL, CHUNK), lambda i, rs, od: (0, od[i])),
                row(ZP_V),
                pl.BlockSpec((CHUNK, LANES), lambda i, rs, od: (od[i], 0)),
                pl.BlockSpec((1, LANES), lambda i, rs, od: (0, 0)),
                pl.BlockSpec((CHUNK, LANES), lambda i, rs, od: (0, 0))]
    ones_col = jnp.zeros((CHUNK, LANES), BF16).at[:, 0].set(1)
    args = [qc, kt, zp, cg, bias, ones_col]
    if reverse:
        hf, ga, hg = extras
        in_specs += [row(0), row(ZS_O), row(ZS_MB), row(0),
                     pl.BlockSpec((M_HEADS, M_HDIM), lambda i, rs, od: (0, 0))]
        args += [hf, zs, zs, ga, hg]
    grid_spec = pltpu.PrefetchScalarGridSpec(
        num_scalar_prefetch=2,
        grid=(nchunks,),
        in_specs=in_specs,
        out_specs=row(0),
        scratch_shapes=[pltpu.VMEM((M_HEADS, M_HDIM, V_AUG), F32),
                        pltpu.VMEM((M_HEADS, 1, LANES), F32)],
    )
    return pl.pallas_call(
        functools.partial(_mlstm_kernel, reverse=reverse),
        grid_spec=grid_spec,
        out_shape=jax.ShapeDtypeStruct((t, D_MODEL), BF16),
        compiler_params=_cparams(("arbitrary",)),
        name="mlstm_bwd" if reverse else "mlstm_fwd",
    )(reset, order, *args)


def _pack_bf16_pair(lo, hi):
    lo_bits = pltpu.bitcast(lo.astype(F32), U32) >> 16
    hi_bits = pltpu.bitcast(hi.astype(F32), U32) & HI_HALF
    return hi_bits | lo_bits


def _unpack_bf16_pair(p):
    lo = pltpu.bitcast(p << 16, F32)
    hi = pltpu.bitcast(p & HI_HALF, F32)
    return lo.astype(BF16), hi.astype(BF16)


def _outproj_kernel(m_ref, xp_ref, xs_ref, w_ref, g_ref, wr_ref, br_ref, x1_ref, hp_ref, lg_ref,
                    *, n_prompt_tiles):
    i = pl.program_id(0)
    y = jnp.dot(m_ref[...], w_ref[...], preferred_element_type=F32)

    def finish(x):
        x1 = x + y
        x1_ref[...] = x1
        hn = _rms(x1, g_ref[...])
        h_hi = hn.astype(BF16)
        h_lo = (hn - h_hi.astype(F32)).astype(BF16)
        hp_ref[...] = _pack_bf16_pair(h_hi[:, :HALF_D], h_hi[:, HALF_D:])
        dot = functools.partial(jnp.dot, preferred_element_type=F32)
        both = dot(h_hi, wr_ref[...])
        lg_ref[...] = both[:, :LANES] + both[:, LANES:] + dot(h_lo, wr_ref[:, :LANES]) + br_ref[...]

    @pl.when(i < n_prompt_tiles)
    def _():
        finish(xp_ref[...])

    @pl.when(i >= n_prompt_tiles)
    def _():
        finish(xs_ref[...])


def _outproj(merged, xp, xs, w, g, wr, br, tm):
    t = merged.shape[0]
    npt = xp.shape[0] // tm
    full = lambda shape: pl.BlockSpec(shape, lambda i: (0, 0))
    wr_hi = wr.astype(BF16)
    wr_lo = (wr - wr_hi.astype(F32)).astype(BF16)
    wr_both = jnp.concatenate([wr_hi, wr_lo], axis=1)
    return pl.pallas_call(
        functools.partial(_outproj_kernel, n_prompt_tiles=npt),
        grid=(t // tm,),
        in_specs=[
            pl.BlockSpec((tm, D_MODEL), lambda i: (i, 0)),
            pl.BlockSpec((tm, D_MODEL), lambda i: (jnp.minimum(i, npt - 1), 0)),
            pl.BlockSpec((tm, D_MODEL), lambda i: (jnp.maximum(i - npt, 0), 0)),
            full((D_MODEL, D_MODEL)), full((1, D_MODEL)), full((D_MODEL, 2 * LANES)),
            full((1, LANES)),
        ],
        out_specs=[
            pl.BlockSpec((tm, D_MODEL), lambda i: (i, 0)),
            pl.BlockSpec((tm, HALF_D), lambda i: (i, 0)),
            pl.BlockSpec((tm, LANES), lambda i: (i, 0)),
        ],
        out_shape=[
            jax.ShapeDtypeStruct((t, D_MODEL), F32),
            jax.ShapeDtypeStruct((t, HALF_D), U32),
            jax.ShapeDtypeStruct((t, LANES), F32),
        ],
        compiler_params=_cparams(("arbitrary",)),
        name="outproj",
    )(merged, xp, xs, w, g, wr_both, br)


R_W0, R_W1, R_E0, R_E1, R_RANK0, R_RANK1 = range(6)


def _route_kernel(lg_ref, rt_ref, cnt_ref, carry_ref, *, rows):
    i = pl.program_id(0)

    @pl.when(i == 0)
    def _():
        carry_ref[...] = jnp.zeros_like(carry_ref)

    lg = lg_ref[...]
    lane = lax.broadcasted_iota(jnp.int32, lg.shape, 1).astype(F32)
    big = np.float32(LANES)
    neg = -jnp.inf
    rmax = lambda a: jnp.max(a, axis=-1, keepdims=True)
    rmin = lambda a: jnp.min(a, axis=-1, keepdims=True)

    gmask = lane < N_GROUPS
    gl = jnp.where(gmask, lg, neg)
    gmax = rmax(gl)
    gsel = rmin(jnp.where(gl == gmax, lane, big))
    p_g = 1.0 / jnp.sum(jnp.where(gmask, jnp.exp(gl - gmax), 0.0), axis=-1, keepdims=True)

    lo = N_GROUPS + EXPERTS_PER_GROUP * gsel
    emask = jnp.logical_and(lane >= lo, lane < lo + EXPERTS_PER_GROUP)
    el = jnp.where(emask, lg, neg)
    v0 = rmax(el)
    i0 = rmin(jnp.where(el == v0, lane, big))
    el1 = jnp.where(lane == i0, neg, el)
    v1 = rmax(el1)
    i1 = rmin(jnp.where(el1 == v1, lane, big))
    ex = jnp.exp(v1 - v0)
    w0 = p_g / (1.0 + ex)
    w1 = p_g * ex / (1.0 + ex)
    e0 = i0 - N_GROUPS
    e1 = i1 - N_GROUPS

    oh0 = lane == e0
    oh1 = lane == e1
    onehot = jnp.logical_or(oh0, oh1)
    r_i = lax.broadcasted_iota(jnp.int32, (rows, rows), 0)
    c_i = lax.broadcasted_iota(jnp.int32, (rows, rows), 1)
    strict = (c_i < r_i).astype(BF16)
    before = jnp.dot(strict, onehot.astype(BF16), preferred_element_type=F32) + carry_ref[0:1, :]
    rank0 = jnp.sum(jnp.where(oh0, before, 0.0), axis=-1, keepdims=True)
    rank1 = jnp.sum(jnp.where(oh1, before, 0.0), axis=-1, keepdims=True)
    total = carry_ref[0:1, :] + jnp.sum(onehot.astype(F32), axis=0, keepdims=True)
    carry_ref[...] = jnp.broadcast_to(total, carry_ref.shape)
    cnt_ref[...] = jnp.broadcast_to(total, cnt_ref.shape)

    rec = jnp.zeros(lg.shape, F32)
    for idx, val in ((R_W0, w0), (R_W1, w1), (R_E0, e0), (R_E1, e1), (R_RANK0, rank0), (R_RANK1, rank1)):
        rec = jnp.where(lane == idx, val, rec)
    rt_ref[...] = rec


def _route(logits, rows):
    t = logits.shape[0]
    return pl.pallas_call(
        functools.partial(_route_kernel, rows=rows),
        grid=(t // rows,),
        in_specs=[pl.BlockSpec((rows, LANES), lambda i: (i, 0))],
        out_specs=[pl.BlockSpec((rows, LANES), lambda i: (i, 0)),
                   pl.BlockSpec((8, LANES), lambda i: (0, 0))],
        out_shape=[jax.ShapeDtypeStruct((t, LANES), F32),
                   jax.ShapeDtypeStruct((8, LANES), F32)],
        scratch_shapes=[pltpu.VMEM((8, LANES), F32)],
        compiler_params=_cparams(("arbitrary",)),
        name="route",
    )(logits)


def _dispatch_kernel(zstart_ref, zcount_ref, nused_ref, dest_ref, hp_ref, xs_hbm, zbuf, sem, zsem,
                     *, tm, bm, nblocks):
    i = pl.program_id(0)

    @pl.when(i == 0)
    def _():
        zbuf[...] = jnp.zeros_like(zbuf)

        def per_expert(e, _):
            start, n = zstart_ref[e], zcount_ref[e]

            def issue(r, _):
                pltpu.make_async_copy(zbuf.at[pl.ds(0, 1), :], xs_hbm.at[pl.ds(start + r, 1), :], zsem).start()
                return 0

            def wait(r, _):
                pltpu.make_async_copy(zbuf.at[pl.ds(0, 1), :], xs_hbm.at[pl.ds(start + r, 1), :], zsem).wait()
                return 0
            lax.fori_loop(0, n, issue, 0)
            lax.fori_loop(0, n, wait, 0)
            return 0
        lax.fori_loop(0, N_EXPERTS, per_expert, 0)

        def tail_copy(b):
            row0 = pl.multiple_of(b * bm, bm)
            return pltpu.make_async_copy(zbuf, xs_hbm.at[pl.ds(row0, bm), :], zsem)

        def tail_issue(b, _):
            tail_copy(b).start()
            return 0

        def tail_wait(b, _):
            tail_copy(b).wait()
            return 0
        lax.fori_loop(nused_ref[0], nblocks, tail_issue, 0)
        lax.fori_loop(nused_ref[0], nblocks, tail_wait, 0)

    def body(r, _):
        for k in range(2):
            pltpu.make_async_copy(hp_ref.at[pl.ds(r, 1), :], xs_hbm.at[pl.ds(dest_ref[0, k, r], 1), :],
                                  sem).start()
        return 0
    lax.fori_loop(0, tm, body, 0, unroll=8)
    for k in range(2):
        pltpu.make_async_copy(hp_ref, xs_hbm.at[pl.ds(0, tm), :], sem).wait()


def _dispatch(hp, dest3, zstart, zcount, n_used, bm, nblocks):
    ntiles, _, tm = dest3.shape
    grid_spec = pltpu.PrefetchScalarGridSpec(
        num_scalar_prefetch=3,
        grid=(ntiles,),
        in_specs=[
            pl.BlockSpec((1, 2, tm), lambda i, zs, zc, nu: (i, 0, 0), memory_space=pltpu.SMEM),
            pl.BlockSpec((tm, HALF_D), lambda i, zs, zc, nu: (i, 0)),
        ],
        out_specs=pl.BlockSpec(memory_space=pl.ANY),
        scratch_shapes=[
            pltpu.VMEM((bm, HALF_D), U32),
            pltpu.SemaphoreType.DMA(()),
            pltpu.SemaphoreType.DMA(()),
        ],
    )
    return pl.pallas_call(
        functools.partial(_dispatch_kernel, tm=tm, bm=bm, nblocks=nblocks),
        grid_spec=grid_spec,
        out_shape=jax.ShapeDtypeStruct((nblocks * bm, HALF_D), U32),
        compiler_params=_cparams(("arbitrary",)),
        name="dispatch",
    )(zstart, zcount, n_used, dest3, hp)


def _experts_kernel(be_ref, nused_ref, xs_ref, wg_ref, wu_ref, wd_ref, out_ref):
    del be_ref
    b = pl.program_id(0)

    @pl.when(b < nused_ref[0])
    def _():
        x_lo, x_hi = _unpack_bf16_pair(xs_ref[...])
        xb = jnp.concatenate([x_lo, x_hi], axis=1)
        hg = jnp.dot(xb, wg_ref[0], preferred_element_type=F32)
        hu = jnp.dot(xb, wu_ref[0], preferred_element_type=F32)
        hmid = (hg * _sigmoid(hg) * hu).astype(BF16)
        o = jnp.dot(hmid, wd_ref[0], preferred_element_type=F32).astype(BF16)
        out_ref[...] = _pack_bf16_pair(o[:, :HALF_D], o[:, HALF_D:])

    @pl.when(b >= nused_ref[0])
    def _():
        out_ref[...] = jnp.zeros_like(out_ref)


def _experts(xs, block_e, n_used, wg, wu, wd, bm):
    nblocks = block_e.shape[0]
    wspec = lambda shape: pl.BlockSpec((1,) + shape, lambda b, be, nu: (be[b], 0, 0))
    grid_spec = pltpu.PrefetchScalarGridSpec(
        num_scalar_prefetch=2,
        grid=(nblocks,),
        in_specs=[
            pl.BlockSpec((bm, HALF_D), lambda b, be, nu: (jnp.minimum(b, nu[0] - 1), 0)),
            wspec((D_MODEL, D_EXPERT)), wspec((D_MODEL, D_EXPERT)), wspec((D_EXPERT, D_MODEL)),
        ],
        out_specs=pl.BlockSpec((bm, HALF_D), lambda b, be, nu: (b, 0)),
    )
    return pl.pallas_call(
        _experts_kernel,
        grid_spec=grid_spec,
        out_shape=jax.ShapeDtypeStruct((nblocks * bm, HALF_D), U32),
        compiler_params=_cparams(("arbitrary",)),
        name="experts",
    )(block_e, n_used, xs, wg, wu, wd)


def _combine_kernel(dcur_ref, dnext_ref, x1_ref, rt_ref, g_ref, eo_hbm, yp_ref, ys_ref, gbuf, gsem,
                    *, tm, n_prompt_tiles, n_tiles):
    i = pl.program_id(0)
    slot = lax.rem(i, 2)

    def start_gather(idx_ref, sl):
        def body(r, _):
            for k in range(2):
                pltpu.make_async_copy(eo_hbm.at[pl.ds(idx_ref[0, k, r], 1), :], gbuf.at[sl, k, pl.ds(r, 1), :],
                                      gsem.at[sl]).start()
            return 0
        lax.fori_loop(0, tm, body, 0, unroll=8)

    @pl.when(i == 0)
    def _():
        start_gather(dcur_ref, 0)

    @pl.when(i + 1 < n_tiles)
    def _():
        start_gather(dnext_ref, 1 - slot)

    pltpu.make_async_copy(gbuf.at[slot], gbuf.at[slot], gsem.at[slot]).wait()
    rt = rt_ref[...]
    w0 = rt[:, R_W0:R_W0 + 1]
    w1 = rt[:, R_W1:R_W1 + 1]
    lo0, hi0 = _unpack_bf16_pair(gbuf[slot, 0])
    lo1, hi1 = _unpack_bf16_pair(gbuf[slot, 1])
    y_lo = x1_ref[:, :HALF_D] + w0 * lo0.astype(F32) + w1 * lo1.astype(F32)
    y_hi = x1_ref[:, HALF_D:] + w0 * hi0.astype(F32) + w1 * hi1.astype(F32)
    ms = (jnp.sum(y_lo * y_lo, axis=-1, keepdims=True) + jnp.sum(y_hi * y_hi, axis=-1, keepdims=True)) / D_MODEL
    scale = lax.rsqrt(ms + EPS)

    def store(y_ref):
        y_ref[:, :HALF_D] = y_lo * scale * g_ref[:, :HALF_D]
        y_ref[:, HALF_D:] = y_hi * scale * g_ref[:, HALF_D:]

    @pl.when(i < n_prompt_tiles)
    def _():
        store(yp_ref)

    @pl.when(i >= n_prompt_tiles)
    def _():
        store(ys_ref)


def _combine(x1, route, eo, dest3, g, tp, ts):
    t = tp + ts
    ntiles, _, tm = dest3.shape
    npt = tp // tm
    return pl.pallas_call(
        functools.partial(_combine_kernel, tm=tm, n_prompt_tiles=npt, n_tiles=ntiles),
        grid=(ntiles,),
        in_specs=[
            pl.BlockSpec((1, 2, tm), lambda i: (i, 0, 0), memory_space=pltpu.SMEM),
            pl.BlockSpec((1, 2, tm), lambda i: (jnp.minimum(i + 1, ntiles - 1), 0, 0), memory_space=pltpu.SMEM),
            pl.BlockSpec((tm, D_MODEL), lambda i: (i, 0)),
            pl.BlockSpec((tm, LANES), lambda i: (i, 0)),
            pl.BlockSpec((1, D_MODEL), lambda i: (0, 0)),
            pl.BlockSpec(memory_space=pl.ANY),
        ],
        out_specs=[
            pl.BlockSpec((tm, D_MODEL), lambda i: (jnp.minimum(i, npt - 1), 0)),
            pl.BlockSpec((tm, D_MODEL), lambda i: (jnp.maximum(i - npt, 0), 0)),
        ],
        out_shape=[jax.ShapeDtypeStruct((tp, D_MODEL), F32),
                   jax.ShapeDtypeStruct((ts, D_MODEL), F32)],
        scratch_shapes=[pltpu.VMEM((2, 2, tm, HALF_D), U32), pltpu.SemaphoreType.DMA((2,))],
        compiler_params=_cparams(("arbitrary",)),
        name="combine",
    )(dest3, dest3, x1, route, g, eo)


def _tile(n, pref):
    t = min(pref, n)
    assert n % t == 0, (n, t)
    return t


def _layer(xp, xs, seq_lens, norm_mix_g, norm_ffn_g, norm_final_g, w_in, b_cg, gmlp_ln_g, gmlp_w_s,
           gmlp_b_s, conv_w, head_g, w_out, w_rg, b_rg, w_re, b_re, w_eg, w_eu, w_ed):
    tp, ts = xp.shape[0], xs.shape[0]
    t = tp + ts
    row2 = lambda a: a.reshape(1, -1).astype(F32)

    w_gelu = w_in[:, :2 * D_MODEL].astype(BF16)
    w_qv = jnp.concatenate([w_in[:, 2 * D_MODEL:3 * D_MODEL], w_in[:, 4 * D_MODEL:5 * D_MODEL]],
                           axis=1).astype(BF16)
    w_kt = w_in[:, 3 * D_MODEL:4 * D_MODEL].T.astype(BF16)
    w_sig = jnp.concatenate([w_in[:, 5 * D_MODEL:OFF_CG], w_in[:, OFF_MERGE:]], axis=1).astype(BF16)
    w_cg = jnp.pad(w_in[:, OFF_CG:OFF_MERGE], ((0, 0), (0, LANES - N_CELL_GATES))).astype(BF16)
    bias_cg = jnp.pad(b_cg.astype(F32), (0, LANES - N_CELL_GATES)).reshape(1, LANES)
    bst = jnp.pad(gmlp_b_s.astype(F32).T, ((0, 0), (0, LANES - G_GROUPS)))
    w_router = jnp.pad(jnp.concatenate([w_rg, w_re], axis=1).astype(F32),
                       ((0, 0), (0, LANES - N_GROUPS - N_EXPERTS)))
    b_router = jnp.pad(jnp.concatenate([b_rg, b_re]).astype(F32),
                       (0, LANES - N_GROUPS - N_EXPERTS)).reshape(1, LANES)

    tm = _tile(np.gcd(tp, ts), 1024)
    conv_rows = _tile(np.gcd.reduce(seq_lens), 512)
    starts = np.cumsum([0] + list(seq_lens))
    n_ct = t // conv_rows
    flags = np.zeros((n_ct, 2), np.int32)
    for s0, s1 in zip(starts[:-1], starts[1:]):
        flags[s0 // conv_rows, 0] = 1
        flags[s1 // conv_rows - 1, 1] = 1
    nchunks = t // CHUNK
    first = np.zeros((nchunks,), np.int32)
    last = np.zeros((nchunks,), np.int32)
    for s0, s1 in zip(starts[:-1], starts[1:]):
        first[s0 // CHUNK] = 1
        last[s1 // CHUNK - 1] = 1
    fwd_order = np.arange(nchunks, dtype=np.int32)
    bwd_order = fwd_order[::-1].copy()

    zs, cg, h = _proj_first(xp, xs, row2(norm_mix_g), w_sig, w_cg, _sigmoid, "proj_sigmoid", tm, 1024)
    zp, kt_raw = _proj_qkv(h, w_qv, w_kt, tm, 1024)
    zg = _proj_act(h, w_gelu, _gelu_tanh, "proj_gelu", tm, 2048)
    ga = _gmlp(zg, zs, row2(gmlp_ln_g), gmlp_w_s.astype(BF16), bst, _tile(np.gcd(tp, ts), 512))
    qc, kt = _conv(zp, kt_raw, conv_w.astype(F32), jnp.asarray(flags), conv_rows, 1024)
    hf = _mlstm(qc, kt, zp, zs, cg, bias_cg, jnp.asarray(first), jnp.asarray(fwd_order), reverse=False)
    merged = _mlstm(qc, kt, zp, zs, cg, bias_cg, jnp.asarray(last[::-1].copy()), jnp.asarray(bwd_order),
                    reverse=True, extras=(hf, ga, head_g.astype(F32)))
    x1, hp, logits = _outproj(merged, xp, xs, w_out.astype(BF16), row2(norm_ffn_g), w_router, b_router,
                              _tile(np.gcd(tp, ts), 512))

    route, counts = _route(logits, _tile(t, 512))
    bm = EXPERT_BLOCK
    nblocks = (2 * t) // bm + N_EXPERTS
    counts = counts[0, :N_EXPERTS].astype(jnp.int32)
    padded = ((counts + bm - 1) // bm) * bm
    pad_end = jnp.cumsum(padded)
    pad_start = pad_end - padded
    e_id = route[:, R_E0:R_E1 + 1].astype(jnp.int32)
    rank = route[:, R_RANK0:R_RANK1 + 1].astype(jnp.int32)
    e_iota = jnp.arange(N_EXPERTS, dtype=jnp.int32)
    start_of = jnp.sum(jnp.where(e_id[:, :, None] == e_iota, pad_start, 0), axis=-1)
    dest = start_of + rank
    block_start = jnp.arange(nblocks, dtype=jnp.int32) * bm
    n_used = (pad_end[-1] // bm).astype(jnp.int32).reshape(1)
    last_start = pad_end[-1] - bm
    block_e = jnp.sum(jnp.minimum(block_start, last_start)[:, None] >= pad_end[None, :], axis=1).astype(jnp.int32)
    tiles = lambda tm: dest.reshape(t // tm, tm, 2).transpose(0, 2, 1)
    xs = _dispatch(hp, tiles(_tile(np.gcd(tp, ts), DISPATCH_ROWS)), pad_start + counts, padded - counts,
                   n_used, bm, nblocks)
    eo = _experts(xs, block_e, n_used, w_eg.astype(BF16), w_eu.astype(BF16), w_ed.astype(BF16), bm)
    return _combine(x1, route, eo, tiles(_tile(np.gcd(tp, ts), COMBINE_ROWS)), row2(norm_final_g), tp, ts)


def kernel(x_prompt, x_sample, norm_mix_g, norm_ffn_g, norm_final_g, w_in, b_cell_gates, gmlp_ln_g, gmlp_w_s,
           gmlp_b_s, mlstm_conv_w, mlstm_head_g, w_out, w_router_group, b_router_group, w_router_expert,
           b_router_expert, w_exp_gate, w_exp_up, w_exp_down):
    bp, sp, d = x_prompt.shape
    bs, ss, _ = x_sample.shape
    seq_lens = [sp] * bp + [ss] * bs
    yp, ys = _layer(
        x_prompt.reshape(bp * sp, d), x_sample.reshape(bs * ss, d), seq_lens,
        norm_mix_g[0], norm_ffn_g[0], norm_final_g, w_in[0], b_cell_gates[0], gmlp_ln_g[0], gmlp_w_s[0],
        gmlp_b_s[0], mlstm_conv_w[0], mlstm_head_g[0], w_out[0], w_router_group[0], b_router_group[0],
        w_router_expert[0], b_router_expert[0], w_exp_gate[0], w_exp_up[0], w_exp_down[0])
    return yp.reshape(bp, sp, d), ys.reshape(bs, ss, d)
```

```python
import functools

import jax
import jax.numpy as jnp
import numpy as np
from jax import lax
from jax.experimental import pallas as pl
from jax.experimental.pallas import tpu as pltpu

F32 = jnp.float32
BF16 = jnp.bfloat16
U32 = jnp.uint32
HI_HALF = np.uint32(0xFFFF0000)

D_MODEL = 2048
HALF_D = D_MODEL // 2
CHUNK = 128
G_GROUPS = 8
G_GDIM = D_MODEL // G_GROUPS
M_HEADS = 8
M_HDIM = D_MODEL // M_HEADS
CONV_W = 5
N_GROUPS = 4
EXPERTS_PER_GROUP = 8
N_EXPERTS = N_GROUPS * EXPERTS_PER_GROUP
D_EXPERT = 1024
EPS = 1e-6
LANES = 128
SUBLANES = 8
N_CELL_GATES = 4 * M_HEADS
OFF_CG = 6 * D_MODEL
OFF_MERGE = OFF_CG + N_CELL_GATES

ZG_GU, ZG_GV = range(2)
ZP_Q, ZP_V = range(2)
ZS_O, ZS_MA, ZS_MB = range(3)
PROJ_SUB = 256

V_AUG = M_HDIM + LANES

VMEM_LIMIT = 56 * 1024 * 1024
PROJ_ROWS = 1024
PROJ_COLS = 1024
PROJ_GELU_COLS = 2048
ROW_TILE = 512
CONV_COLS = 1024
EXPERT_BLOCK = 512
DISPATCH_ROWS = 1024
COMBINE_ROWS = 512


def _cparams(sem):
    return pltpu.CompilerParams(dimension_semantics=sem, vmem_limit_bytes=VMEM_LIMIT)


def _rms(x, g):
    ms = jnp.mean(x * x, axis=-1, keepdims=True)
    return x * lax.rsqrt(ms + EPS) * g


def _sigmoid(x):
    return 1.0 / (1.0 + jnp.exp(-x))


def _gelu_tanh(x):
    c = np.float32(np.sqrt(2.0 / np.pi))
    return 0.5 * x * (1.0 + jnp.tanh(c * (x + np.float32(0.044715) * (x * x * x))))


def _matmul_act(h_ref, w_ref, z_ref, act):
    for c in range(w_ref.shape[1] // PROJ_SUB):
        cs = slice(c * PROJ_SUB, (c + 1) * PROJ_SUB)
        acc = jnp.dot(h_ref[...], w_ref[:, cs], preferred_element_type=F32)
        z_ref[:, cs] = act(acc).astype(BF16)


def _proj_first_kernel(xp_ref, xs_ref, g_ref, w_ref, wcg_ref, z_ref, cg_ref, h_ref, *, n_prompt_tiles, act):
    i = pl.program_id(0)

    @pl.when(pl.program_id(1) == 0)
    def _():
        @pl.when(i < n_prompt_tiles)
        def _():
            h_ref[...] = _rms(xp_ref[...], g_ref[...]).astype(BF16)

        @pl.when(i >= n_prompt_tiles)
        def _():
            h_ref[...] = _rms(xs_ref[...], g_ref[...]).astype(BF16)

        cg_ref[...] = jnp.dot(h_ref[...], wcg_ref[...], preferred_element_type=F32)

    _matmul_act(h_ref, w_ref, z_ref, act)


def _proj_first(xp, xs, g, w, wcg, act, name, tm, tn):
    tp, ts = xp.shape[0], xs.shape[0]
    t = tp + ts
    npt, nst = tp // tm, ts // tm
    return pl.pallas_call(
        functools.partial(_proj_first_kernel, n_prompt_tiles=npt, act=act),
        grid=(npt + nst, w.shape[1] // tn),
        in_specs=[
            pl.BlockSpec((tm, D_MODEL), lambda i, j: (jnp.minimum(i, npt - 1), 0)),
            pl.BlockSpec((tm, D_MODEL), lambda i, j: (jnp.maximum(i - npt, 0), 0)),
            pl.BlockSpec((1, D_MODEL), lambda i, j: (0, 0)),
            pl.BlockSpec((D_MODEL, tn), lambda i, j: (0, j)),
            pl.BlockSpec((D_MODEL, LANES), lambda i, j: (0, 0)),
        ],
        out_specs=[
            pl.BlockSpec((tm, tn), lambda i, j: (i, j)),
            pl.BlockSpec((tm, LANES), lambda i, j: (i, 0)),
            pl.BlockSpec((tm, D_MODEL), lambda i, j: (i, 0)),
        ],
        out_shape=[
            jax.ShapeDtypeStruct((t, w.shape[1]), BF16),
            jax.ShapeDtypeStruct((t, LANES), F32),
            jax.ShapeDtypeStruct((t, D_MODEL), BF16),
        ],
        compiler_params=_cparams(("arbitrary", "arbitrary")),
        name=name,
    )(xp, xs, g, w, wcg)


def _proj_qkv_kernel(h_ref, w_ref, wkt_ref, z_ref, kt_ref, *, n_q_steps):
    j = pl.program_id(1)
    is_k = jnp.logical_and(j >= n_q_steps, j < 2 * n_q_steps)

    @pl.when(is_k)
    def _():
        kt_ref[...] = lax.dot_general(wkt_ref[...], h_ref[...], (((1,), (1,)), ((), ())),
                                      preferred_element_type=F32).astype(BF16)

    @pl.when(jnp.logical_not(is_k))
    def _():
        _matmul_act(h_ref, w_ref, z_ref, lambda a: a)


def _proj_qkv(h, w_qv, wkt, tm, tn):
    t = h.shape[0]
    nq = D_MODEL // tn
    zcol = lambda j: jnp.where(j < nq, j, jnp.where(j < 2 * nq, nq - 1, j - nq))
    ktrow = lambda j: jnp.clip(j - nq, 0, nq - 1)
    return pl.pallas_call(
        functools.partial(_proj_qkv_kernel, n_q_steps=nq),
        grid=(t // tm, 3 * nq),
        in_specs=[
            pl.BlockSpec((tm, D_MODEL), lambda i, j: (i, 0)),
            pl.BlockSpec((D_MODEL, tn), lambda i, j: (0, zcol(j))),
            pl.BlockSpec((tn, D_MODEL), lambda i, j: (ktrow(j), 0)),
        ],
        out_specs=[
            pl.BlockSpec((tm, tn), lambda i, j: (i, zcol(j))),
            pl.BlockSpec((tn, tm), lambda i, j: (ktrow(j), i)),
        ],
        out_shape=[
            jax.ShapeDtypeStruct((t, 2 * D_MODEL), BF16),
            jax.ShapeDtypeStruct((D_MODEL, t), BF16),
        ],
        compiler_params=_cparams(("arbitrary", "arbitrary")),
        name="proj_qkv",
    )(h, w_qv, wkt)


def _proj_act_kernel(h_ref, w_ref, z_ref, *, act):
    _matmul_act(h_ref, w_ref, z_ref, act)


def _proj_act(h, w, act, name, tm, tn):
    t = h.shape[0]
    return pl.pallas_call(
        functools.partial(_proj_act_kernel, act=act),
        grid=(t // tm, w.shape[1] // tn),
        in_specs=[
            pl.BlockSpec((tm, D_MODEL), lambda i, j: (i, 0)),
            pl.BlockSpec((D_MODEL, tn), lambda i, j: (0, j)),
        ],
        out_specs=pl.BlockSpec((tm, tn), lambda i, j: (i, j)),
        out_shape=jax.ShapeDtypeStruct((t, w.shape[1]), BF16),
        compiler_params=_cparams(("arbitrary", "arbitrary")),
        name=name,
    )(h, w)


def _gmlp_kernel(gu_ref, gv_ref, ma_ref, lng_ref, ws_ref, bst_ref, out_ref, vn_ref, *, rows):
    v = gv_ref[...].astype(F32)
    mu = jnp.mean(v, axis=-1, keepdims=True)
    vc = v - mu
    var = jnp.mean(vc * vc, axis=-1, keepdims=True)
    vn_ref[...] = (vc * lax.rsqrt(var + EPS) * lng_ref[...]).astype(BF16)
    for c in range(rows // CHUNK):
        rs = slice(c * CHUNK, (c + 1) * CHUNK)
        for g in range(G_GROUPS):
            cs = slice(g * G_GDIM, (g + 1) * G_GDIM)
            s = jnp.dot(ws_ref[g], vn_ref[rs, cs], preferred_element_type=F32) + bst_ref[:, g:g + 1]
            a = gu_ref[rs, cs].astype(F32) * s
            out_ref[rs, cs] = (ma_ref[rs, cs].astype(F32) * a).astype(BF16)


def _gmlp(zg, zs, lng, ws, bst, rows):
    t = zg.shape[0]
    zspec = lambda b: pl.BlockSpec((rows, D_MODEL), lambda i, b=b: (i, b))
    return pl.pallas_call(
        functools.partial(_gmlp_kernel, rows=rows),
        grid=(t // rows,),
        in_specs=[
            zspec(ZG_GU), zspec(ZG_GV), zspec(ZS_MA),
            pl.BlockSpec((1, D_MODEL), lambda i: (0, 0)),
            pl.BlockSpec((G_GROUPS, CHUNK, CHUNK), lambda i: (0, 0, 0)),
            pl.BlockSpec((CHUNK, LANES), lambda i: (0, 0)),
        ],
        out_specs=pl.BlockSpec((rows, D_MODEL), lambda i: (i, 0)),
        out_shape=jax.ShapeDtypeStruct((t, D_MODEL), BF16),
        scratch_shapes=[pltpu.VMEM((rows, D_MODEL), BF16)],
        compiler_params=_cparams(("arbitrary",)),
        name="gmlp",
    )(zg, zg, zs, lng, ws, bst)


HALO = 16
CONV_CH = 16


def _conv_kernel(flags_ref, qprev_ref, qcur_ref, qnext_ref, wq_ref, kprev_ref, kcur_ref, knext_ref, wk_ref,
                 q_ref, kt_ref, *, rows, n_q_blocks):
    i = pl.program_id(0)
    j = pl.program_id(1)
    keep_prev = (flags_ref[i, 0] == 0).astype(F32)
    keep_next = (flags_ref[i, 1] == 0).astype(F32)
    half = CONV_W // 2

    @pl.when(j < n_q_blocks)
    def _():
        x = qcur_ref[...].astype(F32)
        p = qprev_ref[...].astype(F32) * keep_prev
        n = qnext_ref[...].astype(F32) * keep_next
        row = lax.broadcasted_iota(jnp.int32, x.shape, 0)
        w = wq_ref[...]
        acc = x * w[half:half + 1, :]
        for d in range(1, half + 1):
            xm = pltpu.roll(x, d, 0)
            for r in range(d):
                xm = jnp.where(row == r, p[HALO - d + r:HALO - d + r + 1, :], xm)
            acc = acc + xm * w[half - d:half - d + 1, :]
            xp = pltpu.roll(x, rows - d, 0)
            for r in range(d):
                xp = jnp.where(row == rows - d + r, n[r:r + 1, :], xp)
            acc = acc + xp * w[half + d:half + d + 1, :]
        q_ref[...] = (acc * _sigmoid(acc)).astype(BF16)

    @pl.when(j >= n_q_blocks)
    def _():
        n_lt = rows // LANES
        lane = lax.broadcasted_iota(jnp.int32, (CONV_CH, LANES), 1)

        def body(g, _):
            rs = pl.ds(pl.multiple_of(g * CONV_CH, CONV_CH), CONV_CH)
            tiles = ([kprev_ref[rs, :].astype(F32) * keep_prev]
                     + [kcur_ref[rs, lt * LANES:(lt + 1) * LANES].astype(F32) for lt in range(n_lt)]
                     + [knext_ref[rs, :].astype(F32) * keep_next])
            w = [wk_ref[tap, rs, :] for tap in range(CONV_W)]
            rolled = {}

            def rot(m, d):
                if (m, d) not in rolled:
                    rolled[(m, d)] = pltpu.roll(tiles[m], d % LANES, 1)
                return rolled[(m, d)]

            for lt in range(n_lt):
                c = lt + 1
                acc = tiles[c] * w[half]
                for d in range(1, half + 1):
                    before = jnp.where(lane >= d, rot(c, d), rot(c - 1, d))
                    after = jnp.where(lane < LANES - d, rot(c, -d), rot(c + 1, -d))
                    acc = acc + before * w[half - d] + after * w[half + d]
                y = acc * _sigmoid(acc) * np.float32(M_HDIM ** -0.5)
                kt_ref[rs, lt * LANES:(lt + 1) * LANES] = y.astype(BF16)
            return 0
        lax.fori_loop(0, kcur_ref.shape[0] // CONV_CH, body, 0, unroll=8)


def _conv(zp, kt_raw, w, flags, rows, cols):
    t = zp.shape[0]
    nqb = D_MODEL // cols
    hb = rows // HALO
    nhb = t // HALO
    lb = rows // LANES
    nlb = t // LANES
    w_q = jnp.pad(w[:, :D_MODEL], ((0, SUBLANES - CONV_W), (0, 0)))
    w_k = jnp.broadcast_to(w[:, D_MODEL:, None], (CONV_W, D_MODEL, LANES))
    qcol = lambda j: jnp.minimum(j, nqb - 1)
    krow = lambda j: jnp.maximum(j - nqb, 0)
    kern = functools.partial(_conv_kernel, rows=rows, n_q_blocks=nqb)
    grid_spec = pltpu.PrefetchScalarGridSpec(
        num_scalar_prefetch=1,
        grid=(t // rows, 2 * nqb),
        in_specs=[
            pl.BlockSpec((HALO, cols), lambda i, j, f: (jnp.maximum(i * hb - 1, 0), qcol(j))),
            pl.BlockSpec((rows, cols), lambda i, j, f: (i, qcol(j))),
            pl.BlockSpec((HALO, cols), lambda i, j, f: (jnp.minimum((i + 1) * hb, nhb - 1), qcol(j))),
            pl.BlockSpec((SUBLANES, cols), lambda i, j, f: (0, qcol(j))),
            pl.BlockSpec((cols, LANES), lambda i, j, f: (krow(j), jnp.maximum(i * lb - 1, 0))),
            pl.BlockSpec((cols, rows), lambda i, j, f: (krow(j), i)),
            pl.BlockSpec((cols, LANES), lambda i, j, f: (krow(j), jnp.minimum((i + 1) * lb, nlb - 1))),
            pl.BlockSpec((CONV_W, cols, LANES), lambda i, j, f: (0, krow(j), 0)),
        ],
        out_specs=[
            pl.BlockSpec((rows, cols), lambda i, j, f: (i, qcol(j))),
            pl.BlockSpec((cols, rows), lambda i, j, f: (krow(j), i)),
        ],
    )
    return pl.pallas_call(
        kern,
        grid_spec=grid_spec,
        out_shape=[jax.ShapeDtypeStruct((t, D_MODEL), BF16), jax.ShapeDtypeStruct((D_MODEL, t), BF16)],
        compiler_params=_cparams(("arbitrary", "arbitrary")),
        name="conv",
    )(flags, zp, zp, zp, w_q, kt_raw, kt_raw, kt_raw, w_k)


def _log_sigmoid(x):
    return jnp.minimum(x, 0.0) - jnp.log1p(jnp.exp(-jnp.abs(x)))


def _running_max_lanes(x, reverse):
    n = x.shape[1]
    lane = lax.broadcasted_iota(jnp.int32, x.shape, 1)
    shift = 1
    while shift < n:
        if reverse:
            moved = jnp.where(lane < n - shift, pltpu.roll(x, n - shift, 1), -jnp.inf)
        else:
            moved = jnp.where(lane >= shift, pltpu.roll(x, shift, 1), -jnp.inf)
        x = jnp.maximum(x, moved)
        shift *= 2
    return x


def _mlstm_kernel(reset_ref, order_ref, *refs, reverse):
    del order_ref
    if reverse:
        (q_ref, kt_ref, v_ref, cg_ref, bias_ref, ones_ref, hf_ref, o_ref, mb_ref, ga_ref, hg_ref,
         out_ref, c_ref, m_ref) = refs
    else:
        q_ref, kt_ref, v_ref, cg_ref, bias_ref, ones_ref, out_ref, c_ref, m_ref = refs
    step = pl.program_id(0)

    @pl.when(reset_ref[step] == 1)
    def _():
        c_ref[...] = jnp.zeros_like(c_ref)
        m_ref[...] = jnp.zeros_like(m_ref)

    L = CHUNK
    gates_t = (cg_ref[...] + bias_ref[...]).T
    lf_t = _log_sigmoid(gates_t)
    r_i = lax.broadcasted_iota(jnp.int32, (L, L), 0)
    c_i = lax.broadcasted_iota(jnp.int32, (L, L), 1)
    if reverse:
        valid = c_i >= r_i
        i_off, f_off, last = 2 * M_HEADS, 3 * M_HEADS, 0
    else:
        valid = c_i <= r_i
        i_off, f_off, last = 0, M_HEADS, L - 1
    tri = valid.astype(F32)
    f_rows = lax.dot_general(lf_t, tri, (((1,), (1,)), ((), ())), precision=lax.Precision.HIGHEST,
                             preferred_element_type=F32)[f_off:f_off + M_HEADS, :]
    b_rows = gates_t[i_off:i_off + M_HEADS, :] - f_rows
    cm_rows = _running_max_lanes(b_rows, reverse)
    cols = jnp.concatenate([cm_rows, f_rows, jnp.zeros((L - 2 * M_HEADS, L), F32)], axis=0).T
    heads = range(M_HEADS)
    hsl = [slice(h * M_HDIM, (h + 1) * M_HDIM) for h in heads]
    dot = functools.partial(jnp.dot, preferred_element_type=F32)
    q = [q_ref[:, hsl[h]] for h in heads]
    kt = [kt_ref[hsl[h], :] for h in heads]
    c_old = [c_ref[h] for h in heads]
    s = [dot(q[h], kt[h]) for h in heads]
    qc = [dot(q[h], c_old[h].astype(BF16)) for h in heads]
    ones_blk = ones_ref[...]
    v_aug = [jnp.concatenate([v_ref[:, hsl[h]], ones_blk], axis=1) for h in heads]
    brow = [b_rows[h:h + 1, :] for h in heads]
    f_last = [f_rows[h:h + 1, last:last + 1] for h in heads]
    m_old = [m_ref[h][:, 0:1] for h in heads]
    u = [jnp.maximum(m_old[h], cols[:, h:h + 1]) for h in heads]

    p = [(s[h] * jnp.exp(jnp.where(valid, brow[h] - u[h], -jnp.inf))).astype(BF16) for h in heads]
    nd = [jnp.exp(m_old[h] - u[h]) * qc[h] + dot(p[h], v_aug[h]) for h in heads]

    for h in heads:
        g_row = f_last[h] + brow[h]
        m_new = jnp.maximum(f_last[h] + m_old[h], jnp.max(g_row, axis=-1, keepdims=True))
        a = jnp.exp(f_last[h] + m_old[h] - m_new)
        kwt = (kt[h].astype(F32) * jnp.exp(g_row - m_new)).astype(BF16)
        c_ref[h] = a * c_old[h] + dot(kwt, v_aug[h])
        m_ref[h] = jnp.broadcast_to(m_new, (1, LANES))

    for h in heads:
        f_col = cols[:, M_HEADS + h:M_HEADS + h + 1]
        den = nd[h][:, M_HDIM:M_HDIM + 1]
        hout = nd[h][:, :M_HDIM] * (1.0 / jnp.maximum(jnp.abs(den), jnp.exp(-(f_col + u[h]))))
        if reverse:
            hsum = hf_ref[:, hsl[h]].astype(F32) + hout
            hn = _rms(hsum, hg_ref[h:h + 1, :])
            b = hn * o_ref[:, hsl[h]].astype(F32)
            merged = ga_ref[:, hsl[h]].astype(F32) + mb_ref[:, hsl[h]].astype(F32) * b
            out_ref[:, hsl[h]] = merged.astype(BF16)
        else:
            out_ref[:, hsl[h]] = hout.astype(BF16)


def _mlstm(qc, kt, zp, zs, cg, bias, reset, order, reverse, extras=None):
    t = zp.shape[0]
    nchunks = t // CHUNK
    row = lambda b: pl.BlockSpec((CHUNK, D_MODEL), lambda i, rs, od, b=b: (od[i], b))
    in_specs = [row(0),
                pl.BlockSpec((D_MODEL, CHUNK), lambda i, rs, od: (0, od[i])),
                row(ZP_V),
                pl.BlockSpec((CHUNK, LANES), lambda i, rs, od: (od[i], 0)),
                pl.BlockSpec((1, LANES), lambda i, rs, od: (0, 0)),
                pl.BlockSpec((CHUNK, LANES), lambda i, rs, od: (0, 0))]
    ones_col = jnp.zeros((CHUNK, LANES), BF16).at[:, 0].set(1)
    args = [qc, kt, zp, cg, bias, ones_col]
    if reverse:
        hf, ga, hg = extras
        in_specs += [row(0), row(ZS_O), row(ZS_MB), row(0),
                     pl.BlockSpec((M_HEADS, M_HDIM), lambda i, rs, od: (0, 0))]
        args += [hf, zs, zs, ga, hg]
    grid_spec = pltpu.PrefetchScalarGridSpec(
        num_scalar_prefetch=2,
        grid=(nchunks,),
        in_specs=in_specs,
        out_specs=row(0),
        scratch_shapes=[pltpu.VMEM((M_HEADS, M_HDIM, V_AUG), F32),
                        pltpu.VMEM((M_HEADS, 1, LANES), F32)],
    )
    return pl.pallas_call(
        functools.partial(_mlstm_kernel, reverse=reverse),
        grid_spec=grid_spec,
        out_shape=jax.ShapeDtypeStruct((t, D_MODEL), BF16),
        compiler_params=_cparams(("arbitrary",)),
        name="mlstm_bwd" if reverse else "mlstm_fwd",
    )(reset, order, *args)


def _pack_bf16_pair(lo, hi):
    lo_bits = pltpu.bitcast(lo.astype(F32), U32) >> 16
    hi_bits = pltpu.bitcast(hi.astype(F32), U32) & HI_HALF
    return hi_bits | lo_bits


def _unpack_bf16_pair(p):
    lo = pltpu.bitcast(p << 16, F32)
    hi = pltpu.bitcast(p & HI_HALF, F32)
    return lo.astype(BF16), hi.astype(BF16)


def _outproj_kernel(m_ref, xp_ref, xs_ref, w_ref, g_ref, wr_ref, br_ref, x1_ref, hp_ref, lg_ref,
                    *, n_prompt_tiles):
    i = pl.program_id(0)
    y = jnp.dot(m_ref[...], w_ref[...], preferred_element_type=F32)

    def finish(x):
        x1 = x + y
        x1_ref[...] = x1
        hn = _rms(x1, g_ref[...])
        h_hi = hn.astype(BF16)
        h_lo = (hn - h_hi.astype(F32)).astype(BF16)
        hp_ref[...] = _pack_bf16_pair(h_hi[:, :HALF_D], h_hi[:, HALF_D:])
        dot = functools.partial(jnp.dot, preferred_element_type=F32)
        both = dot(h_hi, wr_ref[...])
        lg_ref[...] = both[:, :LANES] + both[:, LANES:] + dot(h_lo, wr_ref[:, :LANES]) + br_ref[...]

    @pl.when(i < n_prompt_tiles)
    def _():
        finish(xp_ref[...])

    @pl.when(i >= n_prompt_tiles)
    def _():
        finish(xs_ref[...])


def _outproj(merged, xp, xs, w, g, wr, br, tm):
    t = merged.shape[0]
    npt = xp.shape[0] // tm
    full = lambda shape: pl.BlockSpec(shape, lambda i: (0, 0))
    wr_hi = wr.astype(BF16)
    wr_lo = (wr - wr_hi.astype(F32)).astype(BF16)
    wr_both = jnp.concatenate([wr_hi, wr_lo], axis=1)
    return pl.pallas_call(
        functools.partial(_outproj_kernel, n_prompt_tiles=npt),
        grid=(t // tm,),
        in_specs=[
            pl.BlockSpec((tm, D_MODEL), lambda i: (i, 0)),
            pl.BlockSpec((tm, D_MODEL), lambda i: (jnp.minimum(i, npt - 1), 0)),
            pl.BlockSpec((tm, D_MODEL), lambda i: (jnp.maximum(i - npt, 0), 0)),
            full((D_MODEL, D_MODEL)), full((1, D_MODEL)), full((D_MODEL, 2 * LANES)),
            full((1, LANES)),
        ],
        out_specs=[
            pl.BlockSpec((tm, D_MODEL), lambda i: (i, 0)),
            pl.BlockSpec((tm, HALF_D), lambda i: (i, 0)),
            pl.BlockSpec((tm, LANES), lambda i: (i, 0)),
        ],
        out_shape=[
            jax.ShapeDtypeStruct((t, D_MODEL), F32),
            jax.ShapeDtypeStruct((t, HALF_D), U32),
            jax.ShapeDtypeStruct((t, LANES), F32),
        ],
        compiler_params=_cparams(("arbitrary",)),
        name="outproj",
    )(merged, xp, xs, w, g, wr_both, br)


R_W0, R_W1, R_E0, R_E1, R_RANK0, R_RANK1 = range(6)


def _route_kernel(lg_ref, rt_ref, cnt_ref, carry_ref, *, rows):
    i = pl.program_id(0)

    @pl.when(i == 0)
    def _():
        carry_ref[...] = jnp.zeros_like(carry_ref)

    lg = lg_ref[...]
    lane = lax.broadcasted_iota(jnp.int32, lg.shape, 1).astype(F32)
    big = np.float32(LANES)
    neg = -jnp.inf
    rmax = lambda a: jnp.max(a, axis=-1, keepdims=True)
    rmin = lambda a: jnp.min(a, axis=-1, keepdims=True)

    gmask = lane < N_GROUPS
    gl = jnp.where(gmask, lg, neg)
    gmax = rmax(gl)
    gsel = rmin(jnp.where(gl == gmax, lane, big))
    p_g = 1.0 / jnp.sum(jnp.where(gmask, jnp.exp(gl - gmax), 0.0), axis=-1, keepdims=True)

    lo = N_GROUPS + EXPERTS_PER_GROUP * gsel
    emask = jnp.logical_and(lane >= lo, lane < lo + EXPERTS_PER_GROUP)
    el = jnp.where(emask, lg, neg)
    v0 = rmax(el)
    i0 = rmin(jnp.where(el == v0, lane, big))
    el1 = jnp.where(lane == i0, neg, el)
    v1 = rmax(el1)
    i1 = rmin(jnp.where(el1 == v1, lane, big))
    ex = jnp.exp(v1 - v0)
    w0 = p_g / (1.0 + ex)
    w1 = p_g * ex / (1.0 + ex)
    e0 = i0 - N_GROUPS
    e1 = i1 - N_GROUPS

    oh0 = lane == e0
    oh1 = lane == e1
    onehot = jnp.logical_or(oh0, oh1)
    r_i = lax.broadcasted_iota(jnp.int32, (rows, rows), 0)
    c_i = lax.broadcasted_iota(jnp.int32, (rows, rows), 1)
    strict = (c_i < r_i).astype(BF16)
    before = jnp.dot(strict, onehot.astype(BF16), preferred_element_type=F32) + carry_ref[0:1, :]
    rank0 = jnp.sum(jnp.where(oh0, before, 0.0), axis=-1, keepdims=True)
    rank1 = jnp.sum(jnp.where(oh1, before, 0.0), axis=-1, keepdims=True)
    total = carry_ref[0:1, :] + jnp.sum(onehot.astype(F32), axis=0, keepdims=True)
    carry_ref[...] = jnp.broadcast_to(total, carry_ref.shape)
    cnt_ref[...] = jnp.broadcast_to(total, cnt_ref.shape)

    rec = jnp.zeros(lg.shape, F32)
    for idx, val in ((R_W0, w0), (R_W1, w1), (R_E0, e0), (R_E1, e1), (R_RANK0, rank0), (R_RANK1, rank1)):
        rec = jnp.where(lane == idx, val, rec)
    rt_ref[...] = rec


def _route(logits, rows):
    t = logits.shape[0]
    return pl.pallas_call(
        functools.partial(_route_kernel, rows=rows),
        grid=(t // rows,),
        in_specs=[pl.BlockSpec((rows, LANES), lambda i: (i, 0))],
        out_specs=[pl.BlockSpec((rows, LANES), lambda i: (i, 0)),
                   pl.BlockSpec((8, LANES), lambda i: (0, 0))],
        out_shape=[jax.ShapeDtypeStruct((t, LANES), F32),
                   jax.ShapeDtypeStruct((8, LANES), F32)],
        scratch_shapes=[pltpu.VMEM((8, LANES), F32)],
        compiler_params=_cparams(("arbitrary",)),
        name="route",
    )(logits)


def _dispatch_kernel(zstart_ref, zcount_ref, nused_ref, dest_ref, hp_ref, xs_hbm, zbuf, sem, zsem,
                     *, tm, bm, nblocks):
    i = pl.program_id(0)

    @pl.when(i == 0)
    def _():
        zbuf[...] = jnp.zeros_like(zbuf)

        def per_expert(e, _):
            start, n = zstart_ref[e], zcount_ref[e]

            def issue(r, _):
                pltpu.make_async_copy(zbuf.at[pl.ds(0, 1), :], xs_hbm.at[pl.ds(start + r, 1), :], zsem).start()
                return 0

            def wait(r, _):
                pltpu.make_async_copy(zbuf.at[pl.ds(0, 1), :], xs_hbm.at[pl.ds(start + r, 1), :], zsem).wait()
                return 0
            lax.fori_loop(0, n, issue, 0)
            lax.fori_loop(0, n, wait, 0)
            return 0
        lax.fori_loop(0, N_EXPERTS, per_expert, 0)

        def tail_copy(b):
            row0 = pl.multiple_of(b * bm, bm)
            return pltpu.make_async_copy(zbuf, xs_hbm.at[pl.ds(row0, bm), :], zsem)

        def tail_issue(b, _):
            tail_copy(b).start()
            return 0

        def tail_wait(b, _):
            tail_copy(b).wait()
            return 0
        lax.fori_loop(nused_ref[0], nblocks, tail_issue, 0)
        lax.fori_loop(nused_ref[0], nblocks, tail_wait, 0)

    def body(r, _):
        for k in range(2):
            pltpu.make_async_copy(hp_ref.at[pl.ds(r, 1), :], xs_hbm.at[pl.ds(dest_ref[2 * r + k], 1), :],
                                  sem).start()
        return 0
    lax.fori_loop(0, tm, body, 0, unroll=8)
    for k in range(2):
        pltpu.make_async_copy(hp_ref, xs_hbm.at[pl.ds(0, tm), :], sem).wait()


def _dispatch(hp, dest, tm, zstart, zcount, n_used, bm, nblocks):
    ntiles = hp.shape[0] // tm
    grid_spec = pltpu.PrefetchScalarGridSpec(
        num_scalar_prefetch=3,
        grid=(ntiles,),
        in_specs=[
            pl.BlockSpec((2 * tm,), lambda i, zs, zc, nu: (i,), memory_space=pltpu.SMEM),
            pl.BlockSpec((tm, HALF_D), lambda i, zs, zc, nu: (i, 0)),
        ],
        out_specs=pl.BlockSpec(memory_space=pl.ANY),
        scratch_shapes=[
            pltpu.VMEM((bm, HALF_D), U32),
            pltpu.SemaphoreType.DMA(()),
            pltpu.SemaphoreType.DMA(()),
        ],
    )
    return pl.pallas_call(
        functools.partial(_dispatch_kernel, tm=tm, bm=bm, nblocks=nblocks),
        grid_spec=grid_spec,
        out_shape=jax.ShapeDtypeStruct((nblocks * bm, HALF_D), U32),
        compiler_params=_cparams(("arbitrary",)),
        name="dispatch",
    )(zstart, zcount, n_used, dest, hp)


def _experts_kernel(be_ref, nused_ref, xs_ref, wg_ref, wu_ref, wd_ref, out_ref):
    del be_ref
    b = pl.program_id(0)

    @pl.when(b < nused_ref[0])
    def _():
        x_lo, x_hi = _unpack_bf16_pair(xs_ref[...])
        xb = jnp.concatenate([x_lo, x_hi], axis=1)
        hg = jnp.dot(xb, wg_ref[0], preferred_element_type=F32)
        hu = jnp.dot(xb, wu_ref[0], preferred_element_type=F32)
        hmid = (hg * _sigmoid(hg) * hu).astype(BF16)
        o = jnp.dot(hmid, wd_ref[0], preferred_element_type=F32).astype(BF16)
        out_ref[...] = _pack_bf16_pair(o[:, :HALF_D], o[:, HALF_D:])

    @pl.when(b >= nused_ref[0])
    def _():
        out_ref[...] = jnp.zeros_like(out_ref)


def _experts(xs, block_e, n_used, wg, wu, wd, bm):
    nblocks = block_e.shape[0]
    wspec = lambda shape: pl.BlockSpec((1,) + shape, lambda b, be, nu: (be[b], 0, 0))
    grid_spec = pltpu.PrefetchScalarGridSpec(
        num_scalar_prefetch=2,
        grid=(nblocks,),
        in_specs=[
            pl.BlockSpec((bm, HALF_D), lambda b, be, nu: (jnp.minimum(b, nu[0] - 1), 0)),
            wspec((D_MODEL, D_EXPERT)), wspec((D_MODEL, D_EXPERT)), wspec((D_EXPERT, D_MODEL)),
        ],
        out_specs=pl.BlockSpec((bm, HALF_D), lambda b, be, nu: (b, 0)),
    )
    return pl.pallas_call(
        _experts_kernel,
        grid_spec=grid_spec,
        out_shape=jax.ShapeDtypeStruct((nblocks * bm, HALF_D), U32),
        compiler_params=_cparams(("arbitrary",)),
        name="experts",
    )(block_e, n_used, xs, wg, wu, wd)


def _combine_kernel(dcur_ref, dnext_ref, x1_ref, rt_ref, g_ref, eo_hbm, yp_ref, ys_ref, gbuf, gsem,
                    *, tm, n_prompt_tiles, n_tiles):
    i = pl.program_id(0)
    slot = lax.rem(i, 2)

    def start_gather(idx_ref, sl):
        def body(r, _):
            for k in range(2):
                pltpu.make_async_copy(eo_hbm.at[pl.ds(idx_ref[2 * r + k], 1), :], gbuf.at[sl, k, pl.ds(r, 1), :],
                                      gsem.at[sl]).start()
            return 0
        lax.fori_loop(0, tm, body, 0, unroll=8)

    @pl.when(i == 0)
    def _():
        start_gather(dcur_ref, 0)

    @pl.when(i + 1 < n_tiles)
    def _():
        start_gather(dnext_ref, 1 - slot)

    pltpu.make_async_copy(gbuf.at[slot], gbuf.at[slot], gsem.at[slot]).wait()
    rt = rt_ref[...]
    w0 = rt[:, R_W0:R_W0 + 1]
    w1 = rt[:, R_W1:R_W1 + 1]
    lo0, hi0 = _unpack_bf16_pair(gbuf[slot, 0])
    lo1, hi1 = _unpack_bf16_pair(gbuf[slot, 1])
    y_lo = x1_ref[:, :HALF_D] + w0 * lo0.astype(F32) + w1 * lo1.astype(F32)
    y_hi = x1_ref[:, HALF_D:] + w0 * hi0.astype(F32) + w1 * hi1.astype(F32)
    ms = (jnp.sum(y_lo * y_lo, axis=-1, keepdims=True) + jnp.sum(y_hi * y_hi, axis=-1, keepdims=True)) / D_MODEL
    scale = lax.rsqrt(ms + EPS)

    def store(y_ref):
        y_ref[:, :HALF_D] = y_lo * scale * g_ref[:, :HALF_D]
        y_ref[:, HALF_D:] = y_hi * scale * g_ref[:, HALF_D:]

    @pl.when(i < n_prompt_tiles)
    def _():
        store(yp_ref)

    @pl.when(i >= n_prompt_tiles)
    def _():
        store(ys_ref)


def _combine(x1, route, eo, dest, tm, g, tp, ts):
    t = tp + ts
    ntiles = t // tm
    npt = tp // tm
    return pl.pallas_call(
        functools.partial(_combine_kernel, tm=tm, n_prompt_tiles=npt, n_tiles=ntiles),
        grid=(ntiles,),
        in_specs=[
            pl.BlockSpec((2 * tm,), lambda i: (i,), memory_space=pltpu.SMEM),
            pl.BlockSpec((2 * tm,), lambda i: (jnp.minimum(i + 1, ntiles - 1),), memory_space=pltpu.SMEM),
            pl.BlockSpec((tm, D_MODEL), lambda i: (i, 0)),
            pl.BlockSpec((tm, LANES), lambda i: (i, 0)),
            pl.BlockSpec((1, D_MODEL), lambda i: (0, 0)),
            pl.BlockSpec(memory_space=pl.ANY),
        ],
        out_specs=[
            pl.BlockSpec((tm, D_MODEL), lambda i: (jnp.minimum(i, npt - 1), 0)),
            pl.BlockSpec((tm, D_MODEL), lambda i: (jnp.maximum(i - npt, 0), 0)),
        ],
        out_shape=[jax.ShapeDtypeStruct((tp, D_MODEL), F32),
                   jax.ShapeDtypeStruct((ts, D_MODEL), F32)],
        scratch_shapes=[pltpu.VMEM((2, 2, tm, HALF_D), U32), pltpu.SemaphoreType.DMA((2,))],
        compiler_params=_cparams(("arbitrary",)),
        name="combine",
    )(dest, dest, x1, route, g, eo)


def _tile(n, pref):
    t = min(pref, n)
    assert n % t == 0, (n, t)
    return t


def _layer(xp, xs, seq_lens, norm_mix_g, norm_ffn_g, norm_final_g, w_in, b_cg, gmlp_ln_g, gmlp_w_s,
           gmlp_b_s, conv_w, head_g, w_out, w_rg, b_rg, w_re, b_re, w_eg, w_eu, w_ed):
    tp, ts = xp.shape[0], xs.shape[0]
    t = tp + ts
    row2 = lambda a: a.reshape(1, -1).astype(F32)

    w_gelu = w_in[:, :2 * D_MODEL].astype(BF16)
    w_qv = jnp.concatenate([w_in[:, 2 * D_MODEL:3 * D_MODEL], w_in[:, 4 * D_MODEL:5 * D_MODEL]],
                           axis=1).astype(BF16)
    w_kt = w_in[:, 3 * D_MODEL:4 * D_MODEL].T.astype(BF16)
    w_sig = jnp.concatenate([w_in[:, 5 * D_MODEL:OFF_CG], w_in[:, OFF_MERGE:]], axis=1).astype(BF16)
    w_cg = jnp.pad(w_in[:, OFF_CG:OFF_MERGE], ((0, 0), (0, LANES - N_CELL_GATES))).astype(BF16)
    bias_cg = jnp.pad(b_cg.astype(F32), (0, LANES - N_CELL_GATES)).reshape(1, LANES)
    bst = jnp.pad(gmlp_b_s.astype(F32).T, ((0, 0), (0, LANES - G_GROUPS)))
    w_router = jnp.pad(jnp.concatenate([w_rg, w_re], axis=1).astype(F32),
                       ((0, 0), (0, LANES - N_GROUPS - N_EXPERTS)))
    b_router = jnp.pad(jnp.concatenate([b_rg, b_re]).astype(F32),
                       (0, LANES - N_GROUPS - N_EXPERTS)).reshape(1, LANES)

    tm = _tile(np.gcd(tp, ts), PROJ_ROWS)
    row_tile = _tile(np.gcd(tp, ts), ROW_TILE)
    conv_rows = _tile(np.gcd.reduce(seq_lens), ROW_TILE)
    starts = np.cumsum([0] + list(seq_lens))
    n_ct = t // conv_rows
    flags = np.zeros((n_ct, 2), np.int32)
    for s0, s1 in zip(starts[:-1], starts[1:]):
        flags[s0 // conv_rows, 0] = 1
        flags[s1 // conv_rows - 1, 1] = 1
    nchunks = t // CHUNK
    first = np.zeros((nchunks,), np.int32)
    last = np.zeros((nchunks,), np.int32)
    for s0, s1 in zip(starts[:-1], starts[1:]):
        first[s0 // CHUNK] = 1
        last[s1 // CHUNK - 1] = 1
    fwd_order = np.arange(nchunks, dtype=np.int32)
    bwd_order = fwd_order[::-1].copy()

    zs, cg, h = _proj_first(xp, xs, row2(norm_mix_g), w_sig, w_cg, _sigmoid, "proj_sigmoid", tm, PROJ_COLS)
    zp, kt_raw = _proj_qkv(h, w_qv, w_kt, tm, PROJ_COLS)
    zg = _proj_act(h, w_gelu, _gelu_tanh, "proj_gelu", tm, PROJ_GELU_COLS)
    ga = _gmlp(zg, zs, row2(gmlp_ln_g), gmlp_w_s.astype(BF16), bst, row_tile)
    qc, kt = _conv(zp, kt_raw, conv_w.astype(F32), jnp.asarray(flags), conv_rows, CONV_COLS)
    hf = _mlstm(qc, kt, zp, zs, cg, bias_cg, jnp.asarray(first), jnp.asarray(fwd_order), reverse=False)
    merged = _mlstm(qc, kt, zp, zs, cg, bias_cg, jnp.asarray(last[::-1].copy()), jnp.asarray(bwd_order),
                    reverse=True, extras=(hf, ga, head_g.astype(F32)))
    x1, hp, logits = _outproj(merged, xp, xs, w_out.astype(BF16), row2(norm_ffn_g), w_router, b_router, row_tile)

    route, counts = _route(logits, row_tile)
    bm = EXPERT_BLOCK
    nblocks = (2 * t) // bm + N_EXPERTS
    counts = counts[0, :N_EXPERTS].astype(jnp.int32)
    padded = ((counts + bm - 1) // bm) * bm
    pad_end = jnp.cumsum(padded)
    pad_start = pad_end - padded
    e_id = route[:, R_E0:R_E1 + 1].astype(jnp.int32)
    rank = route[:, R_RANK0:R_RANK1 + 1].astype(jnp.int32)
    e_iota = jnp.arange(N_EXPERTS, dtype=jnp.int32)
    start_of = jnp.sum(jnp.where(e_id[:, :, None] == e_iota, pad_start, 0), axis=-1)
    dest = start_of + rank
    block_start = jnp.arange(nblocks, dtype=jnp.int32) * bm
    n_used = (pad_end[-1] // bm).astype(jnp.int32).reshape(1)
    last_start = pad_end[-1] - bm
    block_e = jnp.sum(jnp.minimum(block_start, last_start)[:, None] >= pad_end[None, :], axis=1).astype(jnp.int32)
    dest_flat = dest.reshape(-1)
    xs = _dispatch(hp, dest_flat, _tile(np.gcd(tp, ts), DISPATCH_ROWS), pad_start + counts,
                   padded - counts, n_used, bm, nblocks)
    eo = _experts(xs, block_e, n_used, w_eg.astype(BF16), w_eu.astype(BF16), w_ed.astype(BF16), bm)
    return _combine(x1, route, eo, dest_flat, _tile(np.gcd(tp, ts), COMBINE_ROWS), row2(norm_final_g), tp, ts)


def kernel(x_prompt, x_sample, norm_mix_g, norm_ffn_g, norm_final_g, w_in, b_cell_gates, gmlp_ln_g, gmlp_w_s,
           gmlp_b_s, mlstm_conv_w, mlstm_head_g, w_out, w_router_group, b_router_group, w_router_expert,
           b_router_expert, w_exp_gate, w_exp_up, w_exp_down):
    bp, sp, d = x_prompt.shape
    bs, ss, _ = x_sample.shape
    seq_lens = [sp] * bp + [ss] * bs
    yp, ys = _layer(
        x_prompt.reshape(bp * sp, d), x_sample.reshape(bs * ss, d), seq_lens,
        norm_mix_g[0], norm_ffn_g[0], norm_final_g, w_in[0], b_cell_gates[0], gmlp_ln_g[0], gmlp_w_s[0],
        gmlp_b_s[0], mlstm_conv_w[0], mlstm_head_g[0], w_out[0], w_router_group[0], b_router_group[0],
        w_router_expert[0], b_router_expert[0], w_exp_gate[0], w_exp_up[0], w_exp_down[0])
    return yp.reshape(bp, sp, d), ys.reshape(bs, ss, d)
```

```python
import functools

import jax
import jax.numpy as jnp
import numpy as np
from jax import lax
from jax.experimental import pallas as pl
from jax.experimental.pallas import tpu as pltpu

F32 = jnp.float32
BF16 = jnp.bfloat16
U32 = jnp.uint32
HI_HALF = np.uint32(0xFFFF0000)

D_MODEL = 2048
HALF_D = D_MODEL // 2
CHUNK = 128
G_GROUPS = 8
G_GDIM = D_MODEL // G_GROUPS
M_HEADS = 8
M_HDIM = D_MODEL // M_HEADS
CONV_W = 5
N_GROUPS = 4
EXPERTS_PER_GROUP = 8
N_EXPERTS = N_GROUPS * EXPERTS_PER_GROUP
D_EXPERT = 1024
EPS = 1e-6
LANES = 128
SUBLANES = 8
N_CELL_GATES = 4 * M_HEADS
OFF_CG = 6 * D_MODEL
OFF_MERGE = OFF_CG + N_CELL_GATES

ZG_GU, ZG_GV = range(2)
ZP_Q, ZP_V = range(2)
ZS_O, ZS_MA, ZS_MB = range(3)
PROJ_SUB = 256

V_AUG = M_HDIM + LANES

VMEM_LIMIT = 56 * 1024 * 1024
PROJ_ROWS = 1024
PROJ_COLS = 1024
PROJ_GELU_COLS = 2048
ROW_TILE = 512
CONV_COLS = 1024
EXPERT_BLOCK = 512
DISPATCH_ROWS = 1024
COMBINE_ROWS = 512


def _cparams(sem):
    return pltpu.CompilerParams(dimension_semantics=sem, vmem_limit_bytes=VMEM_LIMIT)


def _rms(x, g):
    ms = jnp.mean(x * x, axis=-1, keepdims=True)
    return x * lax.rsqrt(ms + EPS) * g


def _sigmoid(x):
    return 1.0 / (1.0 + jnp.exp(-x))


def _gelu_tanh(x):
    c = np.float32(np.sqrt(2.0 / np.pi))
    return 0.5 * x * (1.0 + jnp.tanh(c * (x + np.float32(0.044715) * (x * x * x))))


def _matmul_act(h_ref, w_ref, z_ref, act):
    for c in range(w_ref.shape[1] // PROJ_SUB):
        cs = slice(c * PROJ_SUB, (c + 1) * PROJ_SUB)
        acc = jnp.dot(h_ref[...], w_ref[:, cs], preferred_element_type=F32)
        z_ref[:, cs] = act(acc).astype(BF16)


def _proj_first_kernel(xp_ref, xs_ref, g_ref, w_ref, wcg_ref, z_ref, cg_ref, h_ref, *, n_prompt_tiles, act):
    i = pl.program_id(0)

    @pl.when(pl.program_id(1) == 0)
    def _():
        @pl.when(i < n_prompt_tiles)
        def _():
            h_ref[...] = _rms(xp_ref[...], g_ref[...]).astype(BF16)

        @pl.when(i >= n_prompt_tiles)
        def _():
            h_ref[...] = _rms(xs_ref[...], g_ref[...]).astype(BF16)

        cg_ref[...] = jnp.dot(h_ref[...], wcg_ref[...], preferred_element_type=F32)

    _matmul_act(h_ref, w_ref, z_ref, act)


def _proj_first(xp, xs, g, w, wcg, act, name, tm, tn):
    tp, ts = xp.shape[0], xs.shape[0]
    t = tp + ts
    npt, nst = tp // tm, ts // tm
    return pl.pallas_call(
        functools.partial(_proj_first_kernel, n_prompt_tiles=npt, act=act),
        grid=(npt + nst, w.shape[1] // tn),
        in_specs=[
            pl.BlockSpec((tm, D_MODEL), lambda i, j: (jnp.minimum(i, npt - 1), 0)),
            pl.BlockSpec((tm, D_MODEL), lambda i, j: (jnp.maximum(i - npt, 0), 0)),
            pl.BlockSpec((1, D_MODEL), lambda i, j: (0, 0)),
            pl.BlockSpec((D_MODEL, tn), lambda i, j: (0, j)),
            pl.BlockSpec((D_MODEL, LANES), lambda i, j: (0, 0)),
        ],
        out_specs=[
            pl.BlockSpec((tm, tn), lambda i, j: (i, j)),
            pl.BlockSpec((tm, LANES), lambda i, j: (i, 0)),
            pl.BlockSpec((tm, D_MODEL), lambda i, j: (i, 0)),
        ],
        out_shape=[
            jax.ShapeDtypeStruct((t, w.shape[1]), BF16),
            jax.ShapeDtypeStruct((t, LANES), F32),
            jax.ShapeDtypeStruct((t, D_MODEL), BF16),
        ],
        compiler_params=_cparams(("arbitrary", "arbitrary")),
        name=name,
    )(xp, xs, g, w, wcg)


def _proj_qkv_kernel(h_ref, w_ref, wkt_ref, z_ref, kt_ref, *, n_q_steps):
    j = pl.program_id(1)
    is_k = jnp.logical_and(j >= n_q_steps, j < 2 * n_q_steps)

    @pl.when(is_k)
    def _():
        kt_ref[...] = lax.dot_general(wkt_ref[...], h_ref[...], (((1,), (1,)), ((), ())),
                                      preferred_element_type=F32).astype(BF16)

    @pl.when(jnp.logical_not(is_k))
    def _():
        _matmul_act(h_ref, w_ref, z_ref, lambda a: a)


def _proj_qkv(h, w_qv, wkt, tm, tn):
    t = h.shape[0]
    nq = D_MODEL // tn
    zcol = lambda j: jnp.where(j < nq, j, jnp.where(j < 2 * nq, nq - 1, j - nq))
    ktrow = lambda j: jnp.clip(j - nq, 0, nq - 1)
    return pl.pallas_call(
        functools.partial(_proj_qkv_kernel, n_q_steps=nq),
        grid=(t // tm, 3 * nq),
        in_specs=[
            pl.BlockSpec((tm, D_MODEL), lambda i, j: (i, 0)),
            pl.BlockSpec((D_MODEL, tn), lambda i, j: (0, zcol(j))),
            pl.BlockSpec((tn, D_MODEL), lambda i, j: (ktrow(j), 0)),
        ],
        out_specs=[
            pl.BlockSpec((tm, tn), lambda i, j: (i, zcol(j))),
            pl.BlockSpec((tn, tm), lambda i, j: (ktrow(j), i)),
        ],
        out_shape=[
            jax.ShapeDtypeStruct((t, 2 * D_MODEL), BF16),
            jax.ShapeDtypeStruct((D_MODEL, t), BF16),
        ],
        compiler_params=_cparams(("arbitrary", "arbitrary")),
        name="proj_qkv",
    )(h, w_qv, wkt)


def _proj_act_kernel(h_ref, w_ref, z_ref, *, act):
    _matmul_act(h_ref, w_ref, z_ref, act)


def _proj_act(h, w, act, name, tm, tn):
    t = h.shape[0]
    return pl.pallas_call(
        functools.partial(_proj_act_kernel, act=act),
        grid=(t // tm, w.shape[1] // tn),
        in_specs=[
            pl.BlockSpec((tm, D_MODEL), lambda i, j: (i, 0)),
            pl.BlockSpec((D_MODEL, tn), lambda i, j: (0, j)),
        ],
        out_specs=pl.BlockSpec((tm, tn), lambda i, j: (i, j)),
        out_shape=jax.ShapeDtypeStruct((t, w.shape[1]), BF16),
        compiler_params=_cparams(("arbitrary", "arbitrary")),
        name=name,
    )(h, w)


def _gmlp_kernel(gu_ref, gv_ref, ma_ref, lng_ref, ws_ref, bst_ref, out_ref, vn_ref, *, rows):
    v = gv_ref[...].astype(F32)
    mu = jnp.mean(v, axis=-1, keepdims=True)
    vc = v - mu
    var = jnp.mean(vc * vc, axis=-1, keepdims=True)
    vn_ref[...] = (vc * lax.rsqrt(var + EPS) * lng_ref[...]).astype(BF16)
    for c in range(rows // CHUNK):
        rs = slice(c * CHUNK, (c + 1) * CHUNK)
        for g in range(G_GROUPS):
            cs = slice(g * G_GDIM, (g + 1) * G_GDIM)
            s = jnp.dot(ws_ref[g], vn_ref[rs, cs], preferred_element_type=F32) + bst_ref[:, g:g + 1]
            a = gu_ref[rs, cs].astype(F32) * s
            out_ref[rs, cs] = (ma_ref[rs, cs].astype(F32) * a).astype(BF16)


def _gmlp(zg, zs, lng, ws, bst, rows):
    t = zg.shape[0]
    zspec = lambda b: pl.BlockSpec((rows, D_MODEL), lambda i, b=b: (i, b))
    return pl.pallas_call(
        functools.partial(_gmlp_kernel, rows=rows),
        grid=(t // rows,),
        in_specs=[
            zspec(ZG_GU), zspec(ZG_GV), zspec(ZS_MA),
            pl.BlockSpec((1, D_MODEL), lambda i: (0, 0)),
            pl.BlockSpec((G_GROUPS, CHUNK, CHUNK), lambda i: (0, 0, 0)),
            pl.BlockSpec((CHUNK, LANES), lambda i: (0, 0)),
        ],
        out_specs=pl.BlockSpec((rows, D_MODEL), lambda i: (i, 0)),
        out_shape=jax.ShapeDtypeStruct((t, D_MODEL), BF16),
        scratch_shapes=[pltpu.VMEM((rows, D_MODEL), BF16)],
        compiler_params=_cparams(("arbitrary",)),
        name="gmlp",
    )(zg, zg, zs, lng, ws, bst)


HALO = 16
CONV_CH = 16


def _conv_kernel(flags_ref, qprev_ref, qcur_ref, qnext_ref, wq_ref, kprev_ref, kcur_ref, knext_ref, wk_ref,
                 q_ref, kt_ref, *, rows, n_q_blocks):
    i = pl.program_id(0)
    j = pl.program_id(1)
    keep_prev = (flags_ref[i, 0] == 0).astype(F32)
    keep_next = (flags_ref[i, 1] == 0).astype(F32)
    half = CONV_W // 2

    @pl.when(j < n_q_blocks)
    def _():
        x = qcur_ref[...].astype(F32)
        p = qprev_ref[...].astype(F32) * keep_prev
        n = qnext_ref[...].astype(F32) * keep_next
        row = lax.broadcasted_iota(jnp.int32, x.shape, 0)
        w = wq_ref[...]
        acc = x * w[half:half + 1, :]
        for d in range(1, half + 1):
            xm = pltpu.roll(x, d, 0)
            for r in range(d):
                xm = jnp.where(row == r, p[HALO - d + r:HALO - d + r + 1, :], xm)
            acc = acc + xm * w[half - d:half - d + 1, :]
            xp = pltpu.roll(x, rows - d, 0)
            for r in range(d):
                xp = jnp.where(row == rows - d + r, n[r:r + 1, :], xp)
            acc = acc + xp * w[half + d:half + d + 1, :]
        q_ref[...] = (acc * _sigmoid(acc)).astype(BF16)

    @pl.when(j >= n_q_blocks)
    def _():
        n_lt = rows // LANES
        lane = lax.broadcasted_iota(jnp.int32, (CONV_CH, LANES), 1)

        def body(g, _):
            rs = pl.ds(pl.multiple_of(g * CONV_CH, CONV_CH), CONV_CH)
            tiles = ([kprev_ref[rs, :].astype(F32) * keep_prev]
                     + [kcur_ref[rs, lt * LANES:(lt + 1) * LANES].astype(F32) for lt in range(n_lt)]
                     + [knext_ref[rs, :].astype(F32) * keep_next])
            w = [wk_ref[tap, rs, :] for tap in range(CONV_W)]
            rolled = {}

            def rot(m, d):
                if (m, d) not in rolled:
                    rolled[(m, d)] = pltpu.roll(tiles[m], d % LANES, 1)
                return rolled[(m, d)]

            for lt in range(n_lt):
                c = lt + 1
                acc = tiles[c] * w[half]
                for d in range(1, half + 1):
                    before = jnp.where(lane >= d, rot(c, d), rot(c - 1, d))
                    after = jnp.where(lane < LANES - d, rot(c, -d), rot(c + 1, -d))
                    acc = acc + before * w[half - d] + after * w[half + d]
                y = acc * _sigmoid(acc) * np.float32(M_HDIM ** -0.5)
                kt_ref[rs, lt * LANES:(lt + 1) * LANES] = y.astype(BF16)
            return 0
        lax.fori_loop(0, kcur_ref.shape[0] // CONV_CH, body, 0, unroll=8)


def _conv(zp, kt_raw, w, flags, rows, cols):
    t = zp.shape[0]
    nqb = D_MODEL // cols
    hb = rows // HALO
    nhb = t // HALO
    lb = rows // LANES
    nlb = t // LANES
    w_q = jnp.pad(w[:, :D_MODEL], ((0, SUBLANES - CONV_W), (0, 0)))
    w_k = jnp.broadcast_to(w[:, D_MODEL:, None], (CONV_W, D_MODEL, LANES))
    qcol = lambda j: jnp.minimum(j, nqb - 1)
    krow = lambda j: jnp.maximum(j - nqb, 0)
    kern = functools.partial(_conv_kernel, rows=rows, n_q_blocks=nqb)
    grid_spec = pltpu.PrefetchScalarGridSpec(
        num_scalar_prefetch=1,
        grid=(t // rows, 2 * nqb),
        in_specs=[
            pl.BlockSpec((HALO, cols), lambda i, j, f: (jnp.maximum(i * hb - 1, 0), qcol(j))),
            pl.BlockSpec((rows, cols), lambda i, j, f: (i, qcol(j))),
            pl.BlockSpec((HALO, cols), lambda i, j, f: (jnp.minimum((i + 1) * hb, nhb - 1), qcol(j))),
            pl.BlockSpec((SUBLANES, cols), lambda i, j, f: (0, qcol(j))),
            pl.BlockSpec((cols, LANES), lambda i, j, f: (krow(j), jnp.maximum(i * lb - 1, 0))),
            pl.BlockSpec((cols, rows), lambda i, j, f: (krow(j), i)),
            pl.BlockSpec((cols, LANES), lambda i, j, f: (krow(j), jnp.minimum((i + 1) * lb, nlb - 1))),
            pl.BlockSpec((CONV_W, cols, LANES), lambda i, j, f: (0, krow(j), 0)),
        ],
        out_specs=[
            pl.BlockSpec((rows, cols), lambda i, j, f: (i, qcol(j))),
            pl.BlockSpec((cols, rows), lambda i, j, f: (krow(j), i)),
        ],
    )
    return pl.pallas_call(
        kern,
        grid_spec=grid_spec,
        out_shape=[jax.ShapeDtypeStruct((t, D_MODEL), BF16), jax.ShapeDtypeStruct((D_MODEL, t), BF16)],
        compiler_params=_cparams(("arbitrary", "arbitrary")),
        name="conv",
    )(flags, zp, zp, zp, w_q, kt_raw, kt_raw, kt_raw, w_k)


def _log_sigmoid(x):
    return jnp.minimum(x, 0.0) - jnp.log1p(jnp.exp(-jnp.abs(x)))


def _running_max_lanes(x, reverse):
    n = x.shape[1]
    lane = lax.broadcasted_iota(jnp.int32, x.shape, 1)
    shift = 1
    while shift < n:
        if reverse:
            moved = jnp.where(lane < n - shift, pltpu.roll(x, n - shift, 1), -jnp.inf)
        else:
            moved = jnp.where(lane >= shift, pltpu.roll(x, shift, 1), -jnp.inf)
        x = jnp.maximum(x, moved)
        shift *= 2
    return x


def _mlstm_kernel(reset_ref, order_ref, *refs, reverse):
    del order_ref
    if reverse:
        (q_ref, kt_ref, v_ref, cg_ref, bias_ref, ones_ref, hf_ref, o_ref, mb_ref, ga_ref, hg_ref,
         out_ref, c_ref, m_ref) = refs
    else:
        q_ref, kt_ref, v_ref, cg_ref, bias_ref, ones_ref, out_ref, c_ref, m_ref = refs
    step = pl.program_id(0)

    @pl.when(reset_ref[step] == 1)
    def _():
        c_ref[...] = jnp.zeros_like(c_ref)
        m_ref[...] = jnp.zeros_like(m_ref)

    L = CHUNK
    gates_t = (cg_ref[...] + bias_ref[...]).T
    lf_t = _log_sigmoid(gates_t)
    r_i = lax.broadcasted_iota(jnp.int32, (L, L), 0)
    c_i = lax.broadcasted_iota(jnp.int32, (L, L), 1)
    if reverse:
        valid = c_i >= r_i
        i_off, f_off, last = 2 * M_HEADS, 3 * M_HEADS, 0
    else:
        valid = c_i <= r_i
        i_off, f_off, last = 0, M_HEADS, L - 1
    tri = valid.astype(F32)
    f_rows = lax.dot_general(lf_t, tri, (((1,), (1,)), ((), ())), precision=lax.Precision.HIGHEST,
                             preferred_element_type=F32)[f_off:f_off + M_HEADS, :]
    b_rows = gates_t[i_off:i_off + M_HEADS, :] - f_rows
    cm_rows = _running_max_lanes(b_rows, reverse)
    cols = jnp.concatenate([cm_rows, f_rows, jnp.zeros((L - 2 * M_HEADS, L), F32)], axis=0).T
    heads = range(M_HEADS)
    hsl = [slice(h * M_HDIM, (h + 1) * M_HDIM) for h in heads]
    dot = functools.partial(jnp.dot, preferred_element_type=F32)
    q = [q_ref[:, hsl[h]] for h in heads]
    kt = [kt_ref[hsl[h], :] for h in heads]
    c_old = [c_ref[h] for h in heads]
    s = [dot(q[h], kt[h]) for h in heads]
    qc = [dot(q[h], c_old[h].astype(BF16)) for h in heads]
    ones_blk = ones_ref[...]
    v_aug = [jnp.concatenate([v_ref[:, hsl[h]], ones_blk], axis=1) for h in heads]
    brow = [b_rows[h:h + 1, :] for h in heads]
    f_last = [f_rows[h:h + 1, last:last + 1] for h in heads]
    m_old = [m_ref[h][:, 0:1] for h in heads]
    u = [jnp.maximum(m_old[h], cols[:, h:h + 1]) for h in heads]

    for h in heads:
        g_row = f_last[h] + brow[h]
        m_new = jnp.maximum(f_last[h] + m_old[h], jnp.max(g_row, axis=-1, keepdims=True))
        a = jnp.exp(f_last[h] + m_old[h] - m_new)
        kwt = (kt[h].astype(F32) * jnp.exp(g_row - m_new)).astype(BF16)
        c_ref[h] = a * c_old[h] + dot(kwt, v_aug[h])
        m_ref[h] = jnp.broadcast_to(m_new, (1, LANES))

    p = [(s[h] * jnp.exp(jnp.where(valid, brow[h] - u[h], -jnp.inf))).astype(BF16) for h in heads]
    nd = [jnp.exp(m_old[h] - u[h]) * qc[h] + dot(p[h], v_aug[h]) for h in heads]

    for h in heads:
        f_col = cols[:, M_HEADS + h:M_HEADS + h + 1]
        den = nd[h][:, M_HDIM:M_HDIM + 1]
        hout = nd[h][:, :M_HDIM] * (1.0 / jnp.maximum(jnp.abs(den), jnp.exp(-(f_col + u[h]))))
        if reverse:
            hsum = hf_ref[:, hsl[h]].astype(F32) + hout
            hn = _rms(hsum, hg_ref[h:h + 1, :])
            b = hn * o_ref[:, hsl[h]].astype(F32)
            merged = ga_ref[:, hsl[h]].astype(F32) + mb_ref[:, hsl[h]].astype(F32) * b
            out_ref[:, hsl[h]] = merged.astype(BF16)
        else:
            out_ref[:, hsl[h]] = hout.astype(BF16)


def _mlstm(qc, kt, zp, zs, cg, bias, reset, order, reverse, extras=None):
    t = zp.shape[0]
    nchunks = t // CHUNK
    row = lambda b: pl.BlockSpec((CHUNK, D_MODEL), lambda i, rs, od, b=b: (od[i], b))
    in_specs = [row(0),
                pl.BlockSpec((D_MODEL, CHUNK), lambda i, rs, od: (0, od[i])),
                row(ZP_V),
                pl.BlockSpec((CHUNK, LANES), lambda i, rs, od: (od[i], 0)),
                pl.BlockSpec((1, LANES), lambda i, rs, od: (0, 0)),
                pl.BlockSpec((CHUNK, LANES), lambda i, rs, od: (0, 0))]
    ones_col = jnp.zeros((CHUNK, LANES), BF16).at[:, 0].set(1)
    args = [qc, kt, zp, cg, bias, ones_col]
    if reverse:
        hf, ga, hg = extras
        in_specs += [row(0), row(ZS_O), row(ZS_MB), row(0),
                     pl.BlockSpec((M_HEADS, M_HDIM), lambda i, rs, od: (0, 0))]
        args += [hf, zs, zs, ga, hg]
    grid_spec = pltpu.PrefetchScalarGridSpec(
        num_scalar_prefetch=2,
        grid=(nchunks,),
        in_specs=in_specs,
        out_specs=row(0),
        scratch_shapes=[pltpu.VMEM((M_HEADS, M_HDIM, V_AUG), F32),
                        pltpu.VMEM((M_HEADS, 1, LANES), F32)],
    )
    return pl.pallas_call(
        functools.partial(_mlstm_kernel, reverse=reverse),
        grid_spec=grid_spec,
        out_shape=jax.ShapeDtypeStruct((t, D_MODEL), BF16),
        compiler_params=_cparams(("arbitrary",)),
        name="mlstm_bwd" if reverse else "mlstm_fwd",
    )(reset, order, *args)


def _pack_bf16_pair(lo, hi):
    lo_bits = pltpu.bitcast(lo.astype(F32), U32) >> 16
    hi_bits = pltpu.bitcast(hi.astype(F32), U32) & HI_HALF
    return hi_bits | lo_bits


def _unpack_bf16_pair(p):
    lo = pltpu.bitcast(p << 16, F32)
    hi = pltpu.bitcast(p & HI_HALF, F32)
    return lo.astype(BF16), hi.astype(BF16)


def _outproj_kernel(m_ref, xp_ref, xs_ref, w_ref, g_ref, wr_ref, br_ref, x1_ref, hp_ref, lg_ref,
                    *, n_prompt_tiles):
    i = pl.program_id(0)
    y = jnp.dot(m_ref[...], w_ref[...], preferred_element_type=F32)

    def finish(x):
        x1 = x + y
        x1_ref[...] = x1
        hn = _rms(x1, g_ref[...])
        h_hi = hn.astype(BF16)
        h_lo = (hn - h_hi.astype(F32)).astype(BF16)
        hp_ref[...] = _pack_bf16_pair(h_hi[:, :HALF_D], h_hi[:, HALF_D:])
        dot = functools.partial(jnp.dot, preferred_element_type=F32)
        both = dot(h_hi, wr_ref[...])
        lg_ref[...] = both[:, :LANES] + both[:, LANES:] + dot(h_lo, wr_ref[:, :LANES]) + br_ref[...]

    @pl.when(i < n_prompt_tiles)
    def _():
        finish(xp_ref[...])

    @pl.when(i >= n_prompt_tiles)
    def _():
        finish(xs_ref[...])


def _outproj(merged, xp, xs, w, g, wr, br, tm):
    t = merged.shape[0]
    npt = xp.shape[0] // tm
    full = lambda shape: pl.BlockSpec(shape, lambda i: (0, 0))
    wr_hi = wr.astype(BF16)
    wr_lo = (wr - wr_hi.astype(F32)).astype(BF16)
    wr_both = jnp.concatenate([wr_hi, wr_lo], axis=1)
    return pl.pallas_call(
        functools.partial(_outproj_kernel, n_prompt_tiles=npt),
        grid=(t // tm,),
        in_specs=[
            pl.BlockSpec((tm, D_MODEL), lambda i: (i, 0)),
            pl.BlockSpec((tm, D_MODEL), lambda i: (jnp.minimum(i, npt - 1), 0)),
            pl.BlockSpec((tm, D_MODEL), lambda i: (jnp.maximum(i - npt, 0), 0)),
            full((D_MODEL, D_MODEL)), full((1, D_MODEL)), full((D_MODEL, 2 * LANES)),
            full((1, LANES)),
        ],
        out_specs=[
            pl.BlockSpec((tm, D_MODEL), lambda i: (i, 0)),
            pl.BlockSpec((tm, HALF_D), lambda i: (i, 0)),
            pl.BlockSpec((tm, LANES), lambda i: (i, 0)),
        ],
        out_shape=[
            jax.ShapeDtypeStruct((t, D_MODEL), F32),
            jax.ShapeDtypeStruct((t, HALF_D), U32),
            jax.ShapeDtypeStruct((t, LANES), F32),
        ],
        compiler_params=_cparams(("arbitrary",)),
        name="outproj",
    )(merged, xp, xs, w, g, wr_both, br)


R_W0, R_W1, R_E0, R_E1, R_RANK0, R_RANK1 = range(6)


def _route_kernel(lg_ref, rt_ref, cnt_ref, carry_ref, *, rows):
    i = pl.program_id(0)

    @pl.when(i == 0)
    def _():
        carry_ref[...] = jnp.zeros_like(carry_ref)

    lg = lg_ref[...]
    lane = lax.broadcasted_iota(jnp.int32, lg.shape, 1).astype(F32)
    big = np.float32(LANES)
    neg = -jnp.inf
    rmax = lambda a: jnp.max(a, axis=-1, keepdims=True)
    rmin = lambda a: jnp.min(a, axis=-1, keepdims=True)

    gmask = lane < N_GROUPS
    gl = jnp.where(gmask, lg, neg)
    gmax = rmax(gl)
    gsel = rmin(jnp.where(gl == gmax, lane, big))
    p_g = 1.0 / jnp.sum(jnp.where(gmask, jnp.exp(gl - gmax), 0.0), axis=-1, keepdims=True)

    lo = N_GROUPS + EXPERTS_PER_GROUP * gsel
    emask = jnp.logical_and(lane >= lo, lane < lo + EXPERTS_PER_GROUP)
    el = jnp.where(emask, lg, neg)
    v0 = rmax(el)
    i0 = rmin(jnp.where(el == v0, lane, big))
    el1 = jnp.where(lane == i0, neg, el)
    v1 = rmax(el1)
    i1 = rmin(jnp.where(el1 == v1, lane, big))
    ex = jnp.exp(v1 - v0)
    w0 = p_g / (1.0 + ex)
    w1 = p_g * ex / (1.0 + ex)
    e0 = i0 - N_GROUPS
    e1 = i1 - N_GROUPS

    oh0 = lane == e0
    oh1 = lane == e1
    onehot = jnp.logical_or(oh0, oh1)
    r_i = lax.broadcasted_iota(jnp.int32, (rows, rows), 0)
    c_i = lax.broadcasted_iota(jnp.int32, (rows, rows), 1)
    strict = (c_i < r_i).astype(BF16)
    before = jnp.dot(strict, onehot.astype(BF16), preferred_element_type=F32) + carry_ref[0:1, :]
    rank0 = jnp.sum(jnp.where(oh0, before, 0.0), axis=-1, keepdims=True)
    rank1 = jnp.sum(jnp.where(oh1, before, 0.0), axis=-1, keepdims=True)
    total = carry_ref[0:1, :] + jnp.sum(onehot.astype(F32), axis=0, keepdims=True)
    carry_ref[...] = jnp.broadcast_to(total, carry_ref.shape)
    cnt_ref[...] = jnp.broadcast_to(total, cnt_ref.shape)

    rec = jnp.zeros(lg.shape, F32)
    for idx, val in ((R_W0, w0), (R_W1, w1), (R_E0, e0), (R_E1, e1), (R_RANK0, rank0), (R_RANK1, rank1)):
        rec = jnp.where(lane == idx, val, rec)
    rt_ref[...] = rec


def _route(logits, rows):
    t = logits.shape[0]
    return pl.pallas_call(
        functools.partial(_route_kernel, rows=rows),
        grid=(t // rows,),
        in_specs=[pl.BlockSpec((rows, LANES), lambda i: (i, 0))],
        out_specs=[pl.BlockSpec((rows, LANES), lambda i: (i, 0)),
                   pl.BlockSpec((8, LANES), lambda i: (0, 0))],
        out_shape=[jax.ShapeDtypeStruct((t, LANES), F32),
                   jax.ShapeDtypeStruct((8, LANES), F32)],
        scratch_shapes=[pltpu.VMEM((8, LANES), F32)],
        compiler_params=_cparams(("arbitrary",)),
        name="route",
    )(logits)


def _dispatch_kernel(zstart_ref, zcount_ref, nused_ref, dest_ref, hp_ref, xs_hbm, zbuf, sem, zsem,
                     *, tm, bm, nblocks):
    i = pl.program_id(0)

    @pl.when(i == 0)
    def _():
        zbuf[...] = jnp.zeros_like(zbuf)

        def per_expert(e, _):
            start, n = zstart_ref[e], zcount_ref[e]

            def issue(r, _):
                pltpu.make_async_copy(zbuf.at[pl.ds(0, 1), :], xs_hbm.at[pl.ds(start + r, 1), :], zsem).start()
                return 0

            def wait(r, _):
                pltpu.make_async_copy(zbuf.at[pl.ds(0, 1), :], xs_hbm.at[pl.ds(start + r, 1), :], zsem).wait()
                return 0
            lax.fori_loop(0, n, issue, 0)
            lax.fori_loop(0, n, wait, 0)
            return 0
        lax.fori_loop(0, N_EXPERTS, per_expert, 0)

        def tail_copy(b):
            row0 = pl.multiple_of(b * bm, bm)
            return pltpu.make_async_copy(zbuf, xs_hbm.at[pl.ds(row0, bm), :], zsem)

        def tail_issue(b, _):
            tail_copy(b).start()
            return 0

        def tail_wait(b, _):
            tail_copy(b).wait()
            return 0
        lax.fori_loop(nused_ref[0], nblocks, tail_issue, 0)
        lax.fori_loop(nused_ref[0], nblocks, tail_wait, 0)

    def body(r, _):
        for k in range(2):
            pltpu.make_async_copy(hp_ref.at[pl.ds(r, 1), :], xs_hbm.at[pl.ds(dest_ref[2 * r + k], 1), :],
                                  sem).start()
        return 0
    lax.fori_loop(0, tm, body, 0, unroll=8)
    for k in range(2):
        pltpu.make_async_copy(hp_ref, xs_hbm.at[pl.ds(0, tm), :], sem).wait()


def _dispatch(hp, dest, tm, zstart, zcount, n_used, bm, nblocks):
    ntiles = hp.shape[0] // tm
    grid_spec = pltpu.PrefetchScalarGridSpec(
        num_scalar_prefetch=3,
        grid=(ntiles,),
        in_specs=[
            pl.BlockSpec((2 * tm,), lambda i, zs, zc, nu: (i,), memory_space=pltpu.SMEM),
            pl.BlockSpec((tm, HALF_D), lambda i, zs, zc, nu: (i, 0)),
        ],
        out_specs=pl.BlockSpec(memory_space=pl.ANY),
        scratch_shapes=[
            pltpu.VMEM((bm, HALF_D), U32),
            pltpu.SemaphoreType.DMA(()),
            pltpu.SemaphoreType.DMA(()),
        ],
    )
    return pl.pallas_call(
        functools.partial(_dispatch_kernel, tm=tm, bm=bm, nblocks=nblocks),
        grid_spec=grid_spec,
        out_shape=jax.ShapeDtypeStruct((nblocks * bm, HALF_D), U32),
        compiler_params=_cparams(("arbitrary",)),
        name="dispatch",
    )(zstart, zcount, n_used, dest, hp)


def _experts_kernel(be_ref, nused_ref, xs_ref, wg_ref, wu_ref, wd_ref, out_ref):
    del be_ref
    b = pl.program_id(0)

    @pl.when(b < nused_ref[0])
    def _():
        x_lo, x_hi = _unpack_bf16_pair(xs_ref[...])
        xb = jnp.concatenate([x_lo, x_hi], axis=1)
        hg = jnp.dot(xb, wg_ref[0], preferred_element_type=F32)
        hu = jnp.dot(xb, wu_ref[0], preferred_element_type=F32)
        hmid = (hg * _sigmoid(hg) * hu).astype(BF16)
        o = jnp.dot(hmid, wd_ref[0], preferred_element_type=F32).astype(BF16)
        out_ref[...] = _pack_bf16_pair(o[:, :HALF_D], o[:, HALF_D:])

    @pl.when(b >= nused_ref[0])
    def _():
        out_ref[...] = jnp.zeros_like(out_ref)


def _experts(xs, block_e, n_used, wg, wu, wd, bm):
    nblocks = block_e.shape[0]
    wspec = lambda shape: pl.BlockSpec((1,) + shape, lambda b, be, nu: (be[b], 0, 0))
    grid_spec = pltpu.PrefetchScalarGridSpec(
        num_scalar_prefetch=2,
        grid=(nblocks,),
        in_specs=[
            pl.BlockSpec((bm, HALF_D), lambda b, be, nu: (jnp.minimum(b, nu[0] - 1), 0)),
            wspec((D_MODEL, D_EXPERT)), wspec((D_MODEL, D_EXPERT)), wspec((D_EXPERT, D_MODEL)),
        ],
        out_specs=pl.BlockSpec((bm, HALF_D), lambda b, be, nu: (b, 0)),
    )
    return pl.pallas_call(
        _experts_kernel,
        grid_spec=grid_spec,
        out_shape=jax.ShapeDtypeStruct((nblocks * bm, HALF_D), U32),
        compiler_params=_cparams(("arbitrary",)),
        name="experts",
    )(block_e, n_used, xs, wg, wu, wd)


def _combine_kernel(dcur_ref, dnext_ref, x1_ref, rt_ref, g_ref, eo_hbm, yp_ref, ys_ref, gbuf, gsem,
                    *, tm, n_prompt_tiles, n_tiles):
    i = pl.program_id(0)
    slot = lax.rem(i, 2)

    def start_gather(idx_ref, sl):
        def body(r, _):
            for k in range(2):
                pltpu.make_async_copy(eo_hbm.at[pl.ds(idx_ref[2 * r + k], 1), :], gbuf.at[sl, k, pl.ds(r, 1), :],
                                      gsem.at[sl]).start()
            return 0
        lax.fori_loop(0, tm, body, 0, unroll=8)

    @pl.when(i == 0)
    def _():
        start_gather(dcur_ref, 0)

    @pl.when(i + 1 < n_tiles)
    def _():
        start_gather(dnext_ref, 1 - slot)

    pltpu.make_async_copy(gbuf.at[slot], gbuf.at[slot], gsem.at[slot]).wait()
    rt = rt_ref[...]
    w0 = rt[:, R_W0:R_W0 + 1]
    w1 = rt[:, R_W1:R_W1 + 1]
    lo0, hi0 = _unpack_bf16_pair(gbuf[slot, 0])
    lo1, hi1 = _unpack_bf16_pair(gbuf[slot, 1])
    y_lo = x1_ref[:, :HALF_D] + w0 * lo0.astype(F32) + w1 * lo1.astype(F32)
    y_hi = x1_ref[:, HALF_D:] + w0 * hi0.astype(F32) + w1 * hi1.astype(F32)
    ms = (jnp.sum(y_lo * y_lo, axis=-1, keepdims=True) + jnp.sum(y_hi * y_hi, axis=-1, keepdims=True)) / D_MODEL
    scale = lax.rsqrt(ms + EPS)

    def store(y_ref):
        y_ref[:, :HALF_D] = y_lo * scale * g_ref[:, :HALF_D]
        y_ref[:, HALF_D:] = y_hi * scale * g_ref[:, HALF_D:]

    @pl.when(i < n_prompt_tiles)
    def _():
        store(yp_ref)

    @pl.when(i >= n_prompt_tiles)
    def _():
        store(ys_ref)


def _combine(x1, route, eo, dest, tm, g, tp, ts):
    t = tp + ts
    ntiles = t // tm
    npt = tp // tm
    return pl.pallas_call(
        functools.partial(_combine_kernel, tm=tm, n_prompt_tiles=npt, n_tiles=ntiles),
        grid=(ntiles,),
        in_specs=[
            pl.BlockSpec((2 * tm,), lambda i: (i,), memory_space=pltpu.SMEM),
            pl.BlockSpec((2 * tm,), lambda i: (jnp.minimum(i + 1, ntiles - 1),), memory_space=pltpu.SMEM),
            pl.BlockSpec((tm, D_MODEL), lambda i: (i, 0)),
            pl.BlockSpec((tm, LANES), lambda i: (i, 0)),
            pl.BlockSpec((1, D_MODEL), lambda i: (0, 0)),
            pl.BlockSpec(memory_space=pl.ANY),
        ],
        out_specs=[
            pl.BlockSpec((tm, D_MODEL), lambda i: (jnp.minimum(i, npt - 1), 0)),
            pl.BlockSpec((tm, D_MODEL), lambda i: (jnp.maximum(i - npt, 0), 0)),
        ],
        out_shape=[jax.ShapeDtypeStruct((tp, D_MODEL), F32),
                   jax.ShapeDtypeStruct((ts, D_MODEL), F32)],
        scratch_shapes=[pltpu.VMEM((2, 2, tm, HALF_D), U32), pltpu.SemaphoreType.DMA((2,))],
        compiler_params=_cparams(("arbitrary",)),
        name="combine",
    )(dest, dest, x1, route, g, eo)


def _tile(n, pref):
    t = min(pref, n)
    assert n % t == 0, (n, t)
    return t


def _layer(xp, xs, seq_lens, norm_mix_g, norm_ffn_g, norm_final_g, w_in, b_cg, gmlp_ln_g, gmlp_w_s,
           gmlp_b_s, conv_w, head_g, w_out, w_rg, b_rg, w_re, b_re, w_eg, w_eu, w_ed):
    tp, ts = xp.shape[0], xs.shape[0]
    t = tp + ts
    row2 = lambda a: a.reshape(1, -1).astype(F32)

    w_gelu = w_in[:, :2 * D_MODEL].astype(BF16)
    w_qv = jnp.concatenate([w_in[:, 2 * D_MODEL:3 * D_MODEL], w_in[:, 4 * D_MODEL:5 * D_MODEL]],
                           axis=1).astype(BF16)
    w_kt = w_in[:, 3 * D_MODEL:4 * D_MODEL].T.astype(BF16)
    w_sig = jnp.concatenate([w_in[:, 5 * D_MODEL:OFF_CG], w_in[:, OFF_MERGE:]], axis=1).astype(BF16)
    w_cg = jnp.pad(w_in[:, OFF_CG:OFF_MERGE], ((0, 0), (0, LANES - N_CELL_GATES))).astype(BF16)
    bias_cg = jnp.pad(b_cg.astype(F32), (0, LANES - N_CELL_GATES)).reshape(1, LANES)
    bst = jnp.pad(gmlp_b_s.astype(F32).T, ((0, 0), (0, LANES - G_GROUPS)))
    w_router = jnp.pad(jnp.concatenate([w_rg, w_re], axis=1).astype(F32),
                       ((0, 0), (0, LANES - N_GROUPS - N_EXPERTS)))
    b_router = jnp.pad(jnp.concatenate([b_rg, b_re]).astype(F32),
                       (0, LANES - N_GROUPS - N_EXPERTS)).reshape(1, LANES)

    tm = _tile(np.gcd(tp, ts), PROJ_ROWS)
    row_tile = _tile(np.gcd(tp, ts), ROW_TILE)
    conv_rows = _tile(np.gcd.reduce(seq_lens), ROW_TILE)
    starts = np.cumsum([0] + list(seq_lens))
    n_ct = t // conv_rows
    flags = np.zeros((n_ct, 2), np.int32)
    for s0, s1 in zip(starts[:-1], starts[1:]):
        flags[s0 // conv_rows, 0] = 1
        flags[s1 // conv_rows - 1, 1] = 1
    nchunks = t // CHUNK
    first = np.zeros((nchunks,), np.int32)
    last = np.zeros((nchunks,), np.int32)
    for s0, s1 in zip(starts[:-1], starts[1:]):
        first[s0 // CHUNK] = 1
        last[s1 // CHUNK - 1] = 1
    fwd_order = np.arange(nchunks, dtype=np.int32)
    bwd_order = fwd_order[::-1].copy()

    zs, cg, h = _proj_first(xp, xs, row2(norm_mix_g), w_sig, w_cg, _sigmoid, "proj_sigmoid", tm, PROJ_COLS)
    zp, kt_raw = _proj_qkv(h, w_qv, w_kt, tm, PROJ_COLS)
    zg = _proj_act(h, w_gelu, _gelu_tanh, "proj_gelu", tm, PROJ_GELU_COLS)
    ga = _gmlp(zg, zs, row2(gmlp_ln_g), gmlp_w_s.astype(BF16), bst, row_tile)
    qc, kt = _conv(zp, kt_raw, conv_w.astype(F32), jnp.asarray(flags), conv_rows, CONV_COLS)
    hf = _mlstm(qc, kt, zp, zs, cg, bias_cg, jnp.asarray(first), jnp.asarray(fwd_order), reverse=False)
    merged = _mlstm(qc, kt, zp, zs, cg, bias_cg, jnp.asarray(last[::-1].copy()), jnp.asarray(bwd_order),
                    reverse=True, extras=(hf, ga, head_g.astype(F32)))
    x1, hp, logits = _outproj(merged, xp, xs, w_out.astype(BF16), row2(norm_ffn_g), w_router, b_router, row_tile)

    route, counts = _route(logits, row_tile)
    bm = EXPERT_BLOCK
    nblocks = (2 * t) // bm + N_EXPERTS
    counts = counts[0, :N_EXPERTS].astype(jnp.int32)
    padded = ((counts + bm - 1) // bm) * bm
    pad_end = jnp.cumsum(padded)
    pad_start = pad_end - padded
    e_id = route[:, R_E0:R_E1 + 1].astype(jnp.int32)
    rank = route[:, R_RANK0:R_RANK1 + 1].astype(jnp.int32)
    e_iota = jnp.arange(N_EXPERTS, dtype=jnp.int32)
    start_of = jnp.sum(jnp.where(e_id[:, :, None] == e_iota, pad_start, 0), axis=-1)
    dest = start_of + rank
    block_start = jnp.arange(nblocks, dtype=jnp.int32) * bm
    n_used = (pad_end[-1] // bm).astype(jnp.int32).reshape(1)
    last_start = pad_end[-1] - bm
    block_e = jnp.sum(jnp.minimum(block_start, last_start)[:, None] >= pad_end[None, :], axis=1).astype(jnp.int32)
    dest_flat = dest.reshape(-1)
    xs = _dispatch(hp, dest_flat, _tile(np.gcd(tp, ts), DISPATCH_ROWS), pad_start + counts,
                   padded - counts, n_used, bm, nblocks)
    eo = _experts(xs, block_e, n_used, w_eg.astype(BF16), w_eu.astype(BF16), w_ed.astype(BF16), bm)
    return _combine(x1, route, eo, dest_flat, _tile(np.gcd(tp, ts), COMBINE_ROWS), row2(norm_final_g), tp, ts)


def kernel(x_prompt, x_sample, norm_mix_g, norm_ffn_g, norm_final_g, w_in, b_cell_gates, gmlp_ln_g, gmlp_w_s,
           gmlp_b_s, mlstm_conv_w, mlstm_head_g, w_out, w_router_group, b_router_group, w_router_expert,
           b_router_expert, w_exp_gate, w_exp_up, w_exp_down):
    bp, sp, d = x_prompt.shape
    bs, ss, _ = x_sample.shape
    seq_lens = [sp] * bp + [ss] * bs
    yp, ys = _layer(
        x_prompt.reshape(bp * sp, d), x_sample.reshape(bs * ss, d), seq_lens,
        norm_mix_g[0], norm_ffn_g[0], norm_final_g, w_in[0], b_cell_gates[0], gmlp_ln_g[0], gmlp_w_s[0],
        gmlp_b_s[0], mlstm_conv_w[0], mlstm_head_g[0], w_out[0], w_router_group[0], b_router_group[0],
        w_router_expert[0], b_router_expert[0], w_exp_gate[0], w_exp_up[0], w_exp_down[0])
    return yp.reshape(bp, sp, d), ys.reshape(bs, ss, d)
```
